```python
import jax, jax.numpy as jnp
from jax import lax
import numpy as np

D_MODEL = 2048
BATCH = 8
SEQ = 2048
DEPTH = 2
DEC_BATCH = 128
DEC_SEQ = 4
PAST_LEN = 8192
PAGE_SIZE = 128

N_A = DEPTH // 2
N_B = DEPTH - N_A
DH_A = 128
H_A = D_MODEL // DH_A
KV_A = H_A // 4
G_A = H_A // KV_A
W_A = H_A * DH_A
DH_B = 64
H_B = D_MODEL // DH_B
KV_B = H_B // 8
G_B = H_B // KV_B
W_B = H_B * DH_B
WINDOW = 128
BLOCK = 128
EPS = 1e-6
NEG = -1e30

kernel_name = "yoco_stickbreak_swa_sink_step"


def rmsnorm(x, g):
    xf = x.astype(jnp.float32)
    y = xf * lax.rsqrt(jnp.mean(xf * xf, axis=-1, keepdims=True) + EPS)
    return (y * g.astype(jnp.float32)).astype(x.dtype)


def alibi_slopes():
    return jnp.exp2(-8.0 * jnp.arange(1, H_B + 1, dtype=jnp.float32) / H_B).reshape(KV_B, G_B)


def a_proj(h, gain, w_in):
    b, t, _ = h.shape
    p = rmsnorm(h, gain) @ w_in
    q, k, v, g = jnp.split(p, [W_A, W_A + KV_A * DH_A, W_A + 2 * KV_A * DH_A], axis=-1)
    return (q.reshape(b, t, KV_A, G_A, DH_A), k.reshape(b, t, KV_A, DH_A),
            v.reshape(b, t, KV_A, DH_A), g)


def shared_kv(h, gain, w_kv):
    b, t, _ = h.shape
    k, v = jnp.split(rmsnorm(h, gain) @ w_kv, 2, axis=-1)
    return k.reshape(b, t, KV_B, DH_B), v.reshape(b, t, KV_B, DH_B)


def b_proj(h, gain, w_in):
    b, t, _ = h.shape
    q, g = jnp.split(rmsnorm(h, gain) @ w_in, 2, axis=-1)
    return q.reshape(b, t, KV_B, G_B, DH_B), g


def stick_breaking(q, k, v, valid):
    z = jnp.einsum('bqhgd,bkhd->bhgqk', q.astype(jnp.float32), k.astype(jnp.float32)) * (DH_A ** -0.5)
    ls = jax.nn.log_sigmoid(z)
    log1m = jnp.where(valid, ls - z, 0.0)
    after = jnp.sum(log1m, axis=-1, keepdims=True) - jnp.cumsum(log1m, axis=-1)
    w = jnp.where(valid, jnp.exp(ls + after), 0.0)
    return jnp.einsum('bhgqk,bkhd->bqhgd', w, v.astype(jnp.float32))


def sb_prompt(q, k, v):
    b, s = q.shape[:2]
    kpos = jnp.arange(s)

    def blk(i):
        qb = lax.dynamic_slice_in_dim(q, i * BLOCK, BLOCK, axis=1)
        qpos = i * BLOCK + jnp.arange(BLOCK)
        return stick_breaking(qb, k, v, kpos[None, :] < qpos[:, None])

    o = lax.map(blk, jnp.arange(s // BLOCK))
    return jnp.moveaxis(o, 0, 1).reshape(b, s, W_A)


def sb_sample(q, k_new, v_new, cache_k, cache_v, page_table, layer):
    db, ds = q.shape[:2]
    n_past = page_table.shape[1] * cache_k.shape[2]
    kpos = jnp.arange(n_past + ds)
    qpos = n_past + jnp.arange(ds)
    valid = kpos[None, :] < qpos[:, None]

    def one(args):
        qi, ki, vi, pages = args
        kk = jnp.concatenate([cache_k[layer, pages].reshape(n_past, KV_A, DH_A), ki.astype(cache_k.dtype)], 0)
        vv = jnp.concatenate([cache_v[layer, pages].reshape(n_past, KV_A, DH_A), vi.astype(cache_v.dtype)], 0)
        return stick_breaking(qi[None], kk[None], vv[None], valid)[0]

    o = lax.map(one, (q, k_new, v_new, page_table))
    return o.reshape(db, ds, W_A)


def sink_attention(q, k, v, dist, valid, sinks):
    s = jnp.einsum('bqhgd,bkhd->bhgqk', q.astype(jnp.float32), k.astype(jnp.float32)) * (DH_B ** -0.5)
    s = jnp.where(valid, s - alibi_slopes()[:, :, None, None] * dist, NEG)
    sk = sinks.astype(jnp.float32).reshape(KV_B, G_B)[None, :, :, None, None]
    m = jnp.maximum(jnp.max(s, axis=-1, keepdims=True), sk)
    p = jnp.exp(s - m)
    denom = jnp.sum(p, axis=-1, keepdims=True) + jnp.exp(sk - m)
    return jnp.einsum('bhgqk,bkhd->bqhgd', p / denom, v.astype(jnp.float32))


def swa_prompt(q, k, v, sinks):
    b, s = q.shape[:2]
    pad = jnp.zeros((b, BLOCK) + k.shape[2:], k.dtype)
    kp = jnp.concatenate([pad, k], 1)
    vp = jnp.concatenate([pad, v], 1)
    i = jnp.arange(BLOCK)[:, None]
    j = jnp.arange(2 * BLOCK)[None, :]
    dist = i - j + BLOCK
    band = (dist >= 0) & (dist < WINDOW)

    def blk(n):
        qb = lax.dynamic_slice_in_dim(q, n * BLOCK, BLOCK, axis=1)
        kb = lax.dynamic_slice_in_dim(kp, n * BLOCK, 2 * BLOCK, axis=1)
        vb = lax.dynamic_slice_in_dim(vp, n * BLOCK, 2 * BLOCK, axis=1)
        valid = band & ((n > 0) | (j >= BLOCK))
        return sink_attention(qb, kb, vb, dist.astype(jnp.float32), valid, sinks)

    o = lax.map(blk, jnp.arange(s // BLOCK))
    return jnp.moveaxis(o, 0, 1).reshape(b, s, W_B)


def swa_sample(q, k_all, v_all, sinks):
    db, ds = q.shape[:2]
    n_buf = k_all.shape[1] - ds
    dist = (n_buf + jnp.arange(ds))[:, None] - jnp.arange(n_buf + ds)[None, :]
    valid = (dist >= 0) & (dist < WINDOW)
    o = sink_attention(q, k_all, v_all, dist.astype(jnp.float32), valid, sinks)
    return o.reshape(db, ds, W_B)


def setup_inputs(seed: int = 0) -> dict:
    key = jax.random.key(seed)
    ks = jax.random.split(key, 20)
    n_pages = PAST_LEN // PAGE_SIZE
    n_pool = (DEC_BATCH * n_pages * 5) // 4
    w_buf = min(WINDOW, PAST_LEN)
    f32 = jnp.float32
    nrm = lambda k, shape, s=1.0: (jax.random.normal(k, shape, f32) * s).astype(f32)
    page_table = jax.random.permutation(ks[7], n_pool)[:DEC_BATCH * n_pages].reshape(DEC_BATCH, n_pages).astype(jnp.int32)
    return {
        'x_prompt': nrm(ks[0], (BATCH, SEQ, D_MODEL)),
        'x_sample': nrm(ks[1], (DEC_BATCH, DEC_SEQ, D_MODEL)),
        'cache_a_k': nrm(ks[2], (N_A, n_pool, PAGE_SIZE, KV_A, DH_A)),
        'cache_a_v': nrm(ks[3], (N_A, n_pool, PAGE_SIZE, KV_A, DH_A)),
        'cache_b_k': nrm(ks[4], (DEC_BATCH, w_buf, KV_B, DH_B)),
        'cache_b_v': nrm(ks[5], (DEC_BATCH, w_buf, KV_B, DH_B)),
        'page_table': page_table,
        'norm_a': 1.0 + nrm(ks[8], (N_A, D_MODEL), 0.02),
        'w_in_a': nrm(ks[9], (N_A, D_MODEL, 2 * W_A + 2 * KV_A * DH_A), D_MODEL ** -0.5),
        'w_out_a': nrm(ks[10], (N_A, W_A, D_MODEL), W_A ** -0.5),
        'norm_kv': 1.0 + nrm(ks[11], (D_MODEL,), 0.02),
        'w_kv': nrm(ks[12], (D_MODEL, 2 * KV_B * DH_B), D_MODEL ** -0.5),
        'norm_b': 1.0 + nrm(ks[13], (N_B, D_MODEL), 0.02),
        'w_in_b': nrm(ks[14], (N_B, D_MODEL, 2 * W_B), D_MODEL ** -0.5),
        'sinks_b': nrm(ks[15], (N_B, H_B), 1.0),
        'w_out_b': nrm(ks[16], (N_B, W_B, D_MODEL), W_B ** -0.5),
        'norm_f': 1.0 + nrm(ks[17], (D_MODEL,), 0.02),
    }


def reference(x_prompt, x_sample, cache_a_k, cache_a_v, cache_b_k, cache_b_v, page_table,
              norm_a, w_in_a, w_out_a, norm_kv, w_kv, norm_b, w_in_b, sinks_b, w_out_b, norm_f):
    w_buf = cache_b_k.shape[1]

    h = x_prompt
    ak_p, av_p = [], []
    kb_p = vb_p = None
    for l in range(DEPTH):
        if l < N_A:
            q, k, v, g = a_proj(h, norm_a[l], w_in_a[l])
            o = sb_prompt(q, k, v)
            h = h + (o.astype(h.dtype) * jax.nn.silu(g)) @ w_out_a[l]
            ak_p.append(k)
            av_p.append(v)
        else:
            j = l - N_A
            if j == 0:
                kb_p, vb_p = shared_kv(h, norm_kv, w_kv)
            q, g = b_proj(h, norm_b[j], w_in_b[j])
            o = swa_prompt(q, kb_p, vb_p, sinks_b[j])
            h = h + (o.astype(h.dtype) * jax.nn.silu(g)) @ w_out_b[j]
    y_prompt = rmsnorm(h, norm_f)
    seq = x_prompt.shape[1]
    b_k_prompt = kb_p[:, seq - w_buf:]
    b_v_prompt = vb_p[:, seq - w_buf:]

    h = x_sample
    ak_s, av_s = [], []
    kb_s = vb_s = None
    for l in range(DEPTH):
        if l < N_A:
            q, k, v, g = a_proj(h, norm_a[l], w_in_a[l])
            o = sb_sample(q, k, v, cache_a_k, cache_a_v, page_table, l)
            h = h + (o.astype(h.dtype) * jax.nn.silu(g)) @ w_out_a[l]
            ak_s.append(k)
            av_s.append(v)
        else:
            j = l - N_A
            if j == 0:
                k_new, v_new = shared_kv(h, norm_kv, w_kv)
                kb_s = jnp.concatenate([cache_b_k, k_new.astype(cache_b_k.dtype)], 1)
                vb_s = jnp.concatenate([cache_b_v, v_new.astype(cache_b_v.dtype)], 1)
            q, g = b_proj(h, norm_b[j], w_in_b[j])
            o = swa_sample(q, kb_s, vb_s, sinks_b[j])
            h = h + (o.astype(h.dtype) * jax.nn.silu(g)) @ w_out_b[j]
    y_sample = rmsnorm(h, norm_f)
    b_k_sample = kb_s[:, -w_buf:]
    b_v_sample = vb_s[:, -w_buf:]

    a_k_prompt = jnp.stack(ak_p, 0)
    a_v_prompt = jnp.stack(av_p, 0)
    a_k_sample = jnp.stack(ak_s, 0)
    a_v_sample = jnp.stack(av_s, 0)
    return (y_prompt, y_sample, a_k_prompt, a_v_prompt, a_k_sample, a_v_sample,
            b_k_prompt, b_v_prompt, b_k_sample, b_v_sample)
```

```python
import functools

import jax
import jax.numpy as jnp
from jax import lax
from jax.experimental import pallas as pl
from jax.experimental.pallas import tpu as pltpu

F32 = jnp.float32
BF16 = jnp.bfloat16

D_MODEL = 2048
DH_A = 128
KV_A = 4
G_A = 4
DH_B = 64
KV_B = 4
G_B = 8
H_B = KV_B * G_B
WINDOW = 128
EPS = 1e-6
NEG = -1e30

VMEM_LIMIT_BYTES = 56 * 1024 * 1024

SB_TQ = 256
SB_TK = 256
SB_PAGES_PER_STEP = 16
SWA_REQ_PER_STEP = 8


def _cparams(sem):
    return pltpu.CompilerParams(dimension_semantics=sem, vmem_limit_bytes=VMEM_LIMIT_BYTES)


def _nt_dot(a, b):
    return lax.dot_general(a, b, (((1,), (1,)), ((), ())), preferred_element_type=F32)


def _dot(a, b):
    return jnp.dot(a, b, preferred_element_type=F32)


def _silu(g):
    return g * (1.0 / (1.0 + jnp.exp(-g)))


def _suffix_matrix(n):
    row = lax.broadcasted_iota(jnp.int32, (n, n), 0)
    col = lax.broadcasted_iota(jnp.int32, (n, n), 1)
    return jnp.where(row > col, 1.0, 0.0).astype(BF16)


def _split_bf16(x):
    hi = x.astype(BF16)
    lo = (x - hi.astype(F32)).astype(BF16)
    return hi, lo


def _log_beta_terms(z):
    l = jnp.log(1.0 + jnp.exp(-jnp.abs(z)))
    ls = jnp.minimum(z, 0.0) - l
    return ls, ls - z


def _inproj_kernel(x_ref, g_ref, w_ref, o_ref, xn_ref, *, n_gains, split):
    j = pl.program_id(1)

    @pl.when(j == 0)
    def _():
        x = x_ref[...]
        xh = x * lax.rsqrt(jnp.mean(x * x, axis=-1, keepdims=True) + EPS)
        for gi in range(n_gains):
            xn_ref[gi] = (xh * g_ref[gi:gi + 1, :]).astype(BF16)

    if n_gains == 1:
        o_ref[...] = _dot(xn_ref[0], w_ref[...])
    else:
        @pl.when(j < split)
        def _():
            o_ref[...] = _dot(xn_ref[0], w_ref[...])

        @pl.when(j >= split)
        def _():
            o_ref[...] = _dot(xn_ref[1], w_ref[...])


def _inproj(x, gains, w, *, tm, tn, split=0):
    t, d = x.shape
    n = w.shape[1]
    n_gains = gains.shape[0]
    kern = functools.partial(_inproj_kernel, n_gains=n_gains, split=split)
    return pl.pallas_call(
        kern,
        grid=(t // tm, n // tn),
        in_specs=[
            pl.BlockSpec((tm, d), lambda i, j: (i, 0)),
            pl.BlockSpec((n_gains, d), lambda i, j: (0, 0)),
            pl.BlockSpec((d, tn), lambda i, j: (0, j)),
        ],
        out_specs=pl.BlockSpec((tm, tn), lambda i, j: (i, j)),
        out_shape=jax.ShapeDtypeStruct((t, n), F32),
        scratch_shapes=[pltpu.VMEM((n_gains, tm, d), BF16)],
        compiler_params=_cparams(("parallel", "arbitrary")),
        name="inproj",
    )(x, gains, w)


def _outproj_kernel(*refs, gated, final_norm):
    refs = list(refs)
    h_ref = refs.pop(0)
    o_ref = refs.pop(0)
    g_ref = refs.pop(0) if gated else None
    w_ref = refs.pop(0)
    nf_ref = refs.pop(0) if final_norm else None
    out_ref = refs.pop(0)

    if gated:
        og = (o_ref[...] * _silu(g_ref[...])).astype(BF16)
    else:
        og = o_ref[...]
    h = h_ref[...] + _dot(og, w_ref[...])
    if final_norm:
        h = h * lax.rsqrt(jnp.mean(h * h, axis=-1, keepdims=True) + EPS) * nf_ref[...]
    out_ref[...] = h


def _outproj(h, o, gate, w, norm_f, *, tm, gate_col_block=0):
    t, d = h.shape
    gated = gate is not None
    final_norm = norm_f is not None
    args = [h, o]
    in_specs = [
        pl.BlockSpec((tm, d), lambda i: (i, 0)),
        pl.BlockSpec((tm, d), lambda i: (i, 0)),
    ]
    if gated:
        args.append(gate)
        in_specs.append(pl.BlockSpec((tm, d), lambda i: (i, gate_col_block)))
    args.append(w)
    in_specs.append(pl.BlockSpec((d, d), lambda i: (0, 0)))
    if final_norm:
        args.append(norm_f)
        in_specs.append(pl.BlockSpec((1, d), lambda i: (0, 0)))
    kern = functools.partial(_outproj_kernel, gated=gated, final_norm=final_norm)
    return pl.pallas_call(
        kern,
        grid=(t // tm,),
        in_specs=in_specs,
        out_specs=pl.BlockSpec((tm, d), lambda i: (i, 0)),
        out_shape=jax.ShapeDtypeStruct((t, d), F32),
        compiler_params=_cparams(("parallel",)),
        name="outproj",
    )(*args)


def _sb_prompt_kernel(q_ref, k_ref, v_ref, g_ref, o_ref, acc_ref, carry_ref):
    qi = pl.program_id(2)
    q = q_ref[...] * (DH_A ** -0.5)
    qs = jnp.concatenate([q[:, g * DH_A:(g + 1) * DH_A] for g in range(G_A)], axis=0).astype(BF16)
    suffix = _suffix_matrix(SB_TK)
    rows = G_A * SB_TQ

    def tile(kb, masked):
        start = pl.multiple_of(kb * SB_TK, SB_TK)
        k = k_ref[pl.ds(start, SB_TK), :].astype(BF16)
        v = v_ref[pl.ds(start, SB_TK), :].astype(BF16)
        z = _nt_dot(qs, k)
        ls, l1m = _log_beta_terms(z)
        if masked:
            qpos = lax.broadcasted_iota(jnp.int32, (rows, SB_TK), 0) & (SB_TQ - 1)
            kpos = lax.broadcasted_iota(jnp.int32, (rows, SB_TK), 1)
            valid = kpos < qpos
            l1m = jnp.where(valid, l1m, 0.0)
        hi, lo = _split_bf16(l1m)
        after = _dot(hi, suffix) + _dot(lo, suffix) + carry_ref[...]
        w = jnp.exp(ls + after)
        if masked:
            w = jnp.where(valid, w, 0.0)
        acc_ref[...] += _dot(w.astype(BF16), v)
        carry_ref[...] += jnp.sum(l1m, axis=-1, keepdims=True)

    acc_ref[...] = jnp.zeros_like(acc_ref)
    carry_ref[...] = jnp.zeros_like(carry_ref)
    tile(qi, True)

    def body(it, c):
        tile(qi - 1 - it, False)
        return c

    lax.fori_loop(0, qi, body, 0)

    gate = _silu(g_ref[...])
    acc = acc_ref[...]
    o_ref[...] = (jnp.concatenate(
        [acc[g * SB_TQ:(g + 1) * SB_TQ] for g in range(G_A)], axis=1) * gate).astype(BF16)


def _sb_prompt(pa, batch, seq):
    nq = seq // SB_TQ
    wq = G_A * DH_A
    k_col0 = (KV_A * wq) // DH_A
    v_col0 = k_col0 + KV_A
    g_col0 = (KV_A * wq + 2 * KV_A * DH_A) // wq
    return pl.pallas_call(
        _sb_prompt_kernel,
        grid=(batch, KV_A, nq),
        in_specs=[
            pl.BlockSpec((SB_TQ, wq), lambda b, h, i: (b * nq + i, h)),
            pl.BlockSpec((seq, DH_A), lambda b, h, i: (b, k_col0 + h)),
            pl.BlockSpec((seq, DH_A), lambda b, h, i: (b, v_col0 + h)),
            pl.BlockSpec((SB_TQ, wq), lambda b, h, i: (b * nq + i, g_col0 + h)),
        ],
        out_specs=pl.BlockSpec((SB_TQ, wq), lambda b, h, i: (b * nq + i, h)),
        out_shape=jax.ShapeDtypeStruct((batch * seq, KV_A * wq), BF16),
        scratch_shapes=[
            pltpu.VMEM((G_A * SB_TQ, DH_A), F32),
            pltpu.VMEM((G_A * SB_TQ, 1), F32),
        ],
        compiler_params=_cparams(("parallel", "parallel", "arbitrary")),
        name="sb_prompt",
    )(pa, pa, pa, pa)


def _sb_sample_kernel(pt_ref, q_ref, kn_ref, vn_ref, *refs, n_new):
    del pt_ref
    npg = SB_PAGES_PER_STEP
    k_refs = refs[:npg]
    v_refs = refs[npg:2 * npg]
    o_ref = refs[2 * npg]
    acc_ref = refs[2 * npg + 1]
    carry_ref = refs[2 * npg + 2]
    c = pl.program_id(1)
    rows_h = G_A * n_new
    rows = KV_A * rows_h
    scale = DH_A ** -0.5

    @pl.when(c == 0)
    def _():
        t_of_row = lax.broadcasted_iota(jnp.int32, (rows_h, 1), 0) & (n_new - 1)
        for h in range(KV_A):
            qh = q_ref[0, h] * scale
            kn = kn_ref[0, h]
            vn = vn_ref[0, h]
            ls_cols, l1m_cols = [], []
            for j in range(n_new):
                zj = jnp.sum(qh * kn[j:j + 1, :], axis=-1, keepdims=True)
                ls_j, l1m_j = _log_beta_terms(zj)
                ls_cols.append(ls_j)
                l1m_cols.append(jnp.where(t_of_row > j, l1m_j, 0.0))
            acc = jnp.zeros((rows_h, DH_A), F32)
            after = jnp.zeros((rows_h, 1), F32)
            for j in reversed(range(n_new)):
                wj = jnp.where(t_of_row > j, jnp.exp(ls_cols[j] + after), 0.0)
                acc = acc + wj * vn[j:j + 1, :]
                after = after + l1m_cols[j]
            acc_ref[h] = acc
            carry_ref[h * rows_h:(h + 1) * rows_h, :] = after

    suffix = _suffix_matrix(128)
    qb = [(q_ref[0, h] * scale).astype(BF16) for h in range(KV_A)]
    for i in range(npg):
        kp = k_refs[i][0]
        vp = v_refs[i][0]
        z = jnp.concatenate(
            [_nt_dot(qb[h], kp[:, h * DH_A:(h + 1) * DH_A].astype(BF16)) for h in range(KV_A)], axis=0)
        ls, l1m = _log_beta_terms(z)
        hi, lo = _split_bf16(l1m)
        after = _dot(hi, suffix) + _dot(lo, suffix) + carry_ref[...]
        w = jnp.exp(ls + after).astype(BF16)
        for h in range(KV_A):
            acc_ref[h] += _dot(w[h * rows_h:(h + 1) * rows_h], vp[:, h * DH_A:(h + 1) * DH_A].astype(BF16))
        carry_ref[...] += jnp.sum(l1m, axis=-1, keepdims=True)

    @pl.when(c == pl.num_programs(1) - 1)
    def _():
        o_ref[0] = acc_ref[...]


def _sb_sample(q, k_new, v_new, cache_k, cache_v, page_table):
    db, _, rows_h, _ = q.shape
    ds = k_new.shape[2]
    n_pages = page_table.shape[1]
    page = cache_k.shape[1]
    npg = SB_PAGES_PER_STEP
    n_chunks = n_pages // npg
    wkv = KV_A * DH_A

    def page_spec(i):
        return pl.BlockSpec((1, page, wkv),
                            lambda r, c, pt: (pt[r, n_pages - 1 - (c * npg + i)], 0, 0))

    in_specs = [
        pl.BlockSpec((1, KV_A, rows_h, DH_A), lambda r, c, pt: (r, 0, 0, 0)),
        pl.BlockSpec((1, KV_A, ds, DH_A), lambda r, c, pt: (r, 0, 0, 0)),
        pl.BlockSpec((1, KV_A, ds, DH_A), lambda r, c, pt: (r, 0, 0, 0)),
    ] + [page_spec(i) for i in range(npg)] + [page_spec(i) for i in range(npg)]
    grid_spec = pltpu.PrefetchScalarGridSpec(
        num_scalar_prefetch=1,
        grid=(db, n_chunks),
        in_specs=in_specs,
        out_specs=pl.BlockSpec((1, KV_A, rows_h, DH_A), lambda r, c, pt: (r, 0, 0, 0)),
        scratch_shapes=[
            pltpu.VMEM((KV_A, rows_h, DH_A), F32),
            pltpu.VMEM((KV_A * rows_h, 1), F32),
        ],
    )
    kern = functools.partial(_sb_sample_kernel, n_new=ds)
    return pl.pallas_call(
        kern,
        grid_spec=grid_spec,
        out_shape=jax.ShapeDtypeStruct((db, KV_A, rows_h, DH_A), F32),
        compiler_params=_cparams(("parallel", "arbitrary")),
        name="sb_sample",
    )(page_table, q, k_new, v_new, *([cache_k] * npg), *([cache_v] * npg))


def _sink_attention(qs, k, v, dist, valid, slope, sink):
    s = _nt_dot(qs, k)
    s = jnp.where(valid, s - slope * dist, NEG)
    m = jnp.maximum(jnp.max(s, axis=-1, keepdims=True), sink)
    p = jnp.exp(s - m)
    denom = jnp.sum(p, axis=-1, keepdims=True) + jnp.exp(sink - m)
    return _dot(p.astype(BF16), v) * (1.0 / denom)


def _head_columns(sinks_ref, kvh, rows_per_head):
    g_idx = lax.broadcasted_iota(jnp.int32, (G_B * rows_per_head, 1), 0) // rows_per_head
    slope = jnp.zeros((G_B * rows_per_head, 1), F32)
    sink = jnp.zeros((G_B * rows_per_head, 1), F32)
    for g in range(G_B):
        h = kvh * G_B + g
        slope = jnp.where(g_idx == g, 2.0 ** (-8.0 * (h + 1) / H_B), slope)
        sink = jnp.where(g_idx == g, sinks_ref[h], sink)
    return slope, sink


def _swa_prompt_kernel(sinks_ref, q_ref, g_ref, kvp_ref, kvc_ref, o_ref):
    n = pl.program_id(1)
    blk = WINDOW
    rows = G_B * blk
    q = q_ref[...] * (DH_B ** -0.5)
    gate = _silu(g_ref[...])
    kvp = kvp_ref[...]
    kvc = kvc_ref[...]
    i = lax.broadcasted_iota(jnp.int32, (rows, 2 * blk), 0) & (blk - 1)
    j = lax.broadcasted_iota(jnp.int32, (rows, 2 * blk), 1)
    dist = i - j + blk
    valid = (dist >= 0) & (dist < WINDOW) & ((n > 0) | (j >= blk))
    distf = dist.astype(F32)
    vo = KV_B * DH_B
    outs = []
    for kvh in range(KV_B):
        c0 = kvh * DH_B
        k = jnp.concatenate([kvp[:, c0:c0 + DH_B], kvc[:, c0:c0 + DH_B]], axis=0).astype(BF16)
        v = jnp.concatenate([kvp[:, vo + c0:vo + c0 + DH_B], kvc[:, vo + c0:vo + c0 + DH_B]], axis=0).astype(BF16)
        qs = jnp.concatenate(
            [q[:, (kvh * G_B + g) * DH_B:(kvh * G_B + g + 1) * DH_B] for g in range(G_B)], axis=0).astype(BF16)
        slope, sink = _head_columns(sinks_ref, kvh, blk)
        o = _sink_attention(qs, k, v, distf, valid, slope, sink)
        outs.extend(o[g * blk:(g + 1) * blk] for g in range(G_B))
    o_ref[...] = (jnp.concatenate(outs, axis=1) * gate).astype(BF16)


def _swa_prompt(pb, sinks, batch, seq):
    blk = WINDOW
    nb = seq // blk
    wq = H_B * DH_B
    wkv = 2 * KV_B * DH_B
    kv_col = (2 * wq) // wkv
    return pl.pallas_call(
        _swa_prompt_kernel,
        grid=(batch, nb),
        in_specs=[
            pl.BlockSpec(memory_space=pltpu.SMEM),
            pl.BlockSpec((blk, wq), lambda b, n: (b * nb + n, 0)),
            pl.BlockSpec((blk, wq), lambda b, n: (b * nb + n, 1)),
            pl.BlockSpec((blk, wkv), lambda b, n: (b * nb + jnp.maximum(n - 1, 0), kv_col)),
            pl.BlockSpec((blk, wkv), lambda b, n: (b * nb + n, kv_col)),
        ],
        out_specs=pl.BlockSpec((blk, wq), lambda b, n: (b * nb + n, 0)),
        out_shape=jax.ShapeDtypeStruct((batch * seq, wq), BF16),
        compiler_params=_cparams(("parallel", "arbitrary")),
        name="swa_prompt",
    )(sinks, pb, pb, pb, pb)


def _swa_sample_kernel(sinks_ref, q_ref, kc_ref, vc_ref, kn_ref, vn_ref, o_ref, *, n_new):
    n_buf = kc_ref.shape[1]
    rows = G_B * n_new
    keys = 2 * n_buf
    t = lax.broadcasted_iota(jnp.int32, (rows, keys), 0) & (n_new - 1)
    j = lax.broadcasted_iota(jnp.int32, (rows, keys), 1)
    dist = n_buf + t - j
    valid = (dist >= 0) & (dist < WINDOW)
    distf = dist.astype(F32)
    pad = jnp.zeros((n_buf - kn_ref.shape[1], DH_B), F32)
    for r in range(SWA_REQ_PER_STEP):
        kc = kc_ref[r]
        vc = vc_ref[r]
        kn = kn_ref[r]
        vn = vn_ref[r]
        for kvh in range(KV_B):
            c0 = kvh * DH_B
            k = jnp.concatenate([kc[:, c0:c0 + DH_B], kn[:, c0:c0 + DH_B], pad], axis=0).astype(BF16)
            v = jnp.concatenate([vc[:, c0:c0 + DH_B], vn[:, c0:c0 + DH_B], pad], axis=0).astype(BF16)
            qs = (q_ref[r, kvh] * (DH_B ** -0.5)).astype(BF16)
            slope, sink = _head_columns(sinks_ref, kvh, n_new)
            o_ref[r, kvh] = _sink_attention(qs, k, v, distf, valid, slope, sink)


def _swa_sample(q, cache_k, cache_v, k_new, v_new, sinks):
    db, _, rows, _ = q.shape
    n_buf = cache_k.shape[1]
    ds = k_new.shape[1]
    wkv = KV_B * DH_B
    rq = SWA_REQ_PER_STEP
    ds_pad = -(-ds // 8) * 8
    k_new = jnp.pad(k_new, ((0, 0), (0, ds_pad - ds), (0, 0)))
    v_new = jnp.pad(v_new, ((0, 0), (0, ds_pad - ds), (0, 0)))
    kern = functools.partial(_swa_sample_kernel, n_new=ds)
    return pl.pallas_call(
        kern,
        grid=(db // rq,),
        in_specs=[
            pl.BlockSpec(memory_space=pltpu.SMEM),
            pl.BlockSpec((rq, KV_B, rows, DH_B), lambda i: (i, 0, 0, 0)),
            pl.BlockSpec((rq, n_buf, wkv), lambda i: (i, 0, 0)),
            pl.BlockSpec((rq, n_buf, wkv), lambda i: (i, 0, 0)),
            pl.BlockSpec((rq, ds_pad, wkv), lambda i: (i, 0, 0)),
            pl.BlockSpec((rq, ds_pad, wkv), lambda i: (i, 0, 0)),
        ],
        out_specs=pl.BlockSpec((rq, KV_B, rows, DH_B), lambda i: (i, 0, 0, 0)),
        out_shape=jax.ShapeDtypeStruct((db, KV_B, rows, DH_B), F32),
        compiler_params=_cparams(("parallel",)),
        name="swa_sample",
    )(sinks, q, cache_k, cache_v, k_new, v_new)


def kernel(x_prompt, x_sample, cache_a_k, cache_a_v, cache_b_k, cache_b_v, page_table,
           norm_a, w_in_a, w_out_a, norm_kv, w_kv, norm_b, w_in_b, sinks_b, w_out_b, norm_f):
    batch, seq, d = x_prompt.shape
    db, ds, _ = x_sample.shape
    w_buf = cache_b_k.shape[1]
    n_pool, page = cache_a_k.shape[1], cache_a_k.shape[2]
    wq_a = KV_A * G_A * DH_A
    wkv_a = KV_A * DH_A
    wq_b = H_B * DH_B
    wkv_b = KV_B * DH_B

    w_in_a0 = w_in_a[0].astype(BF16)
    w_out_a0 = w_out_a[0].astype(BF16)
    w_in_bkv = jnp.concatenate([w_in_b[0], w_kv], axis=1).astype(BF16)
    w_out_b0 = w_out_b[0].astype(BF16)
    gains_b = jnp.stack([norm_b[0], norm_kv], axis=0)
    nf = norm_f.reshape(1, d)
    sinks = sinks_b[0]
    split_b = (2 * wq_b) // 512

    xp = x_prompt.reshape(batch * seq, d)
    pa = _inproj(xp, norm_a[0:1], w_in_a0, tm=1024, tn=512)
    a_k_prompt = pa[:, wq_a:wq_a + wkv_a].reshape(1, batch, seq, KV_A, DH_A)
    a_v_prompt = pa[:, wq_a + wkv_a:wq_a + 2 * wkv_a].reshape(1, batch, seq, KV_A, DH_A)
    og = _sb_prompt(pa, batch, seq)
    h1 = _outproj(xp, og, None, w_out_a0, None, tm=512)
    pb = _inproj(h1, gains_b, w_in_bkv, tm=1024, tn=512, split=split_b)
    kvb = pb[:, 2 * wq_b:].reshape(batch, seq, 2 * wkv_b)
    b_k_prompt = kvb[:, seq - w_buf:, :wkv_b].reshape(batch, w_buf, KV_B, DH_B)
    b_v_prompt = kvb[:, seq - w_buf:, wkv_b:].reshape(batch, w_buf, KV_B, DH_B)
    og2 = _swa_prompt(pb, sinks, batch, seq)
    y_prompt = _outproj(h1, og2, None, w_out_b0, nf, tm=512).reshape(batch, seq, d)

    ts = db * ds
    xs = x_sample.reshape(ts, d)
    pas = _inproj(xs, norm_a[0:1], w_in_a0, tm=ts, tn=512)
    k_new_a = pas[:, wq_a:wq_a + wkv_a]
    v_new_a = pas[:, wq_a + wkv_a:wq_a + 2 * wkv_a]
    a_k_sample = k_new_a.reshape(1, db, ds, KV_A, DH_A)
    a_v_sample = v_new_a.reshape(1, db, ds, KV_A, DH_A)
    q_s = pas[:, :wq_a].reshape(db, ds, KV_A, G_A, DH_A).transpose(0, 2, 3, 1, 4).reshape(db, KV_A, G_A * ds, DH_A)
    kn_s = k_new_a.reshape(db, ds, KV_A, DH_A).transpose(0, 2, 1, 3)
    vn_s = v_new_a.reshape(db, ds, KV_A, DH_A).transpose(0, 2, 1, 3)
    o_s = _sb_sample(q_s, kn_s, vn_s,
                     cache_a_k[0].reshape(n_pool, page, wkv_a), cache_a_v[0].reshape(n_pool, page, wkv_a),
                     page_table)
    o_s = o_s.reshape(db, KV_A, G_A, ds, DH_A).transpose(0, 3, 1, 2, 4).reshape(ts, wq_a)
    h1s = _outproj(xs, o_s, pas[:, wq_a + 2 * wkv_a:], w_out_a0, None, tm=ts)
    pbs = _inproj(h1s, gains_b, w_in_bkv, tm=ts, tn=512, split=split_b)
    k_new_b = pbs[:, 2 * wq_b:2 * wq_b + wkv_b].reshape(db, ds, wkv_b)
    v_new_b = pbs[:, 2 * wq_b + wkv_b:].reshape(db, ds, wkv_b)
    q_bs = pbs[:, :wq_b].reshape(db, ds, KV_B, G_B, DH_B).transpose(0, 2, 3, 1, 4).reshape(db, KV_B, G_B * ds, DH_B)
    o_bs = _swa_sample(q_bs, cache_b_k.reshape(db, w_buf, wkv_b), cache_b_v.reshape(db, w_buf, wkv_b),
                       k_new_b, v_new_b, sinks)
    o_bs = o_bs.reshape(db, KV_B, G_B, ds, DH_B).transpose(0, 3, 1, 2, 4).reshape(ts, wq_b)
    y_sample = _outproj(h1s, o_bs, pbs, w_out_b0, nf, tm=ts, gate_col_block=1).reshape(db, ds, d)

    kb_all = jnp.concatenate([cache_b_k, k_new_b.reshape(db, ds, KV_B, DH_B)], axis=1)
    vb_all = jnp.concatenate([cache_b_v, v_new_b.reshape(db, ds, KV_B, DH_B)], axis=1)
    b_k_sample = kb_all[:, -w_buf:]
    b_v_sample = vb_all[:, -w_buf:]

    return (y_prompt, y_sample, a_k_prompt, a_v_prompt, a_k_sample, a_v_sample,
            b_k_prompt, b_v_prompt, b_k_sample, b_v_sample)
```

```python
import functools

import jax
import jax.numpy as jnp
from jax import lax
from jax.experimental import pallas as pl
from jax.experimental.pallas import tpu as pltpu

F32 = jnp.float32
BF16 = jnp.bfloat16

D_MODEL = 2048
DH_A = 128
KV_A = 4
G_A = 4
DH_B = 64
KV_B = 4
G_B = 8
H_B = KV_B * G_B
WINDOW = 128
EPS = 1e-6
NEG = -1e30

VMEM_LIMIT_BYTES = 56 * 1024 * 1024

SB_TQ = 256
SB_TK = 256
SB_PAGES_PER_STEP = 16
PAGE = 128
PAGE_ROWS = PAGE * KV_A
SWA_REQ_PER_STEP = 8


def _cparams(sem):
    return pltpu.CompilerParams(dimension_semantics=sem, vmem_limit_bytes=VMEM_LIMIT_BYTES)


def _nt_dot(a, b):
    return lax.dot_general(a, b, (((1,), (1,)), ((), ())), preferred_element_type=F32)


def _dot(a, b):
    return jnp.dot(a, b, preferred_element_type=F32)


def _silu(g):
    return g * (1.0 / (1.0 + jnp.exp(-g)))


def _suffix_matrix(n):
    row = lax.broadcasted_iota(jnp.int32, (n, n), 0)
    col = lax.broadcasted_iota(jnp.int32, (n, n), 1)
    return jnp.where(row > col, 1.0, 0.0).astype(BF16)


LOG2E = 1.4426950408889634


def _log2_beta_terms(z2):
    neg_abs = lax.bitcast_convert_type(
        lax.bitcast_convert_type(z2, jnp.uint32) | jnp.uint32(0x80000000), F32)
    l = jnp.log2(1.0 + jnp.exp2(neg_abs))
    ls = jnp.minimum(z2, 0.0) - l
    return ls, ls - z2


def _inproj_kernel(x_ref, g_ref, w_ref, o_ref, xn_ref, *, n_gains, split):
    j = pl.program_id(1)

    @pl.when(j == 0)
    def _():
        x = x_ref[...]
        xh = x * lax.rsqrt(jnp.mean(x * x, axis=-1, keepdims=True) + EPS)
        for gi in range(n_gains):
            xn_ref[gi] = (xh * g_ref[gi:gi + 1, :]).astype(BF16)

    if n_gains == 1:
        o_ref[...] = _dot(xn_ref[0], w_ref[...])
    else:
        @pl.when(j < split)
        def _():
            o_ref[...] = _dot(xn_ref[0], w_ref[...])

        @pl.when(j >= split)
        def _():
            o_ref[...] = _dot(xn_ref[1], w_ref[...])


def _inproj(x, gains, w, *, tm, tn, split=0):
    t, d = x.shape
    n = w.shape[1]
    n_gains = gains.shape[0]
    kern = functools.partial(_inproj_kernel, n_gains=n_gains, split=split)
    return pl.pallas_call(
        kern,
        grid=(t // tm, n // tn),
        in_specs=[
            pl.BlockSpec((tm, d), lambda i, j: (i, 0)),
            pl.BlockSpec((n_gains, d), lambda i, j: (0, 0)),
            pl.BlockSpec((d, tn), lambda i, j: (0, j)),
        ],
        out_specs=pl.BlockSpec((tm, tn), lambda i, j: (i, j)),
        out_shape=jax.ShapeDtypeStruct((t, n), F32),
        scratch_shapes=[pltpu.VMEM((n_gains, tm, d), BF16)],
        compiler_params=_cparams(("parallel", "arbitrary")),
        name="inproj",
    )(x, gains, w)


def _outproj_kernel(*refs, gated, final_norm):
    refs = list(refs)
    h_ref = refs.pop(0)
    o_ref = refs.pop(0)
    g_ref = refs.pop(0) if gated else None
    w_ref = refs.pop(0)
    nf_ref = refs.pop(0) if final_norm else None
    out_ref = refs.pop(0)

    if gated:
        og = (o_ref[...] * _silu(g_ref[...])).astype(BF16)
    else:
        og = o_ref[...]
    h = h_ref[...] + _dot(og, w_ref[...])
    if final_norm:
        h = h * lax.rsqrt(jnp.mean(h * h, axis=-1, keepdims=True) + EPS) * nf_ref[...]
    out_ref[...] = h


def _outproj(h, o, gate, w, norm_f, *, tm, gate_col_block=0):
    t, d = h.shape
    gated = gate is not None
    final_norm = norm_f is not None
    args = [h, o]
    in_specs = [
        pl.BlockSpec((tm, d), lambda i: (i, 0)),
        pl.BlockSpec((tm, d), lambda i: (i, 0)),
    ]
    if gated:
        args.append(gate)
        in_specs.append(pl.BlockSpec((tm, d), lambda i: (i, gate_col_block)))
    args.append(w)
    in_specs.append(pl.BlockSpec((d, d), lambda i: (0, 0)))
    if final_norm:
        args.append(norm_f)
        in_specs.append(pl.BlockSpec((1, d), lambda i: (0, 0)))
    kern = functools.partial(_outproj_kernel, gated=gated, final_norm=final_norm)
    return pl.pallas_call(
        kern,
        grid=(t // tm,),
        in_specs=in_specs,
        out_specs=pl.BlockSpec((tm, d), lambda i: (i, 0)),
        out_shape=jax.ShapeDtypeStruct((t, d), F32),
        compiler_params=_cparams(("parallel",)),
        name="outproj",
    )(*args)


def _sb_prompt_kernel(q_ref, k_ref, v_ref, g_ref, o_ref, acc_ref, carry_ref):
    qi = pl.program_id(2)
    q = q_ref[...] * (DH_A ** -0.5 * LOG2E)
    qs = jnp.concatenate([q[:, g * DH_A:(g + 1) * DH_A] for g in range(G_A)], axis=0).astype(BF16)
    suffix = _suffix_matrix(SB_TK)
    rows = G_A * SB_TQ

    def tile(kb, masked):
        start = pl.multiple_of(kb * SB_TK, SB_TK)
        k = k_ref[pl.ds(start, SB_TK), :].astype(BF16)
        v = v_ref[pl.ds(start, SB_TK), :].astype(BF16)
        z = _nt_dot(qs, k)
        ls, l1m = _log2_beta_terms(z)
        if masked:
            qpos = lax.broadcasted_iota(jnp.int32, (rows, SB_TK), 0) & (SB_TQ - 1)
            kpos = lax.broadcasted_iota(jnp.int32, (rows, SB_TK), 1)
            valid = kpos < qpos
            l1m = jnp.where(valid, l1m, 0.0)
        after = _dot(l1m.astype(BF16), suffix)
        w = jnp.exp2(ls + after + carry_ref[...])
        if masked:
            w = jnp.where(valid, w, 0.0)
        acc_ref[...] += _dot(w.astype(BF16), v)
        carry_ref[...] += after[:, 0:1] + l1m[:, 0:1]

    acc_ref[...] = jnp.zeros_like(acc_ref)
    carry_ref[...] = jnp.zeros_like(carry_ref)
    tile(qi, True)

    def body(it, c):
        tile(qi - 1 - it, False)
        return c

    lax.fori_loop(0, qi, body, 0)

    gate = _silu(g_ref[...])
    acc = acc_ref[...]
    o_ref[...] = (jnp.concatenate(
        [acc[g * SB_TQ:(g + 1) * SB_TQ] for g in range(G_A)], axis=1) * gate).astype(BF16)


def _sb_prompt(pa, batch, seq):
    nq = seq // SB_TQ
    wq = G_A * DH_A
    k_col0 = (KV_A * wq) // DH_A
    v_col0 = k_col0 + KV_A
    g_col0 = (KV_A * wq + 2 * KV_A * DH_A) // wq
    return pl.pallas_call(
        _sb_prompt_kernel,
        grid=(batch, KV_A, nq),
        in_specs=[
            pl.BlockSpec((SB_TQ, wq), lambda b, h, i: (b * nq + i, h)),
            pl.BlockSpec((seq, DH_A), lambda b, h, i: (b, k_col0 + h)),
            pl.BlockSpec((seq, DH_A), lambda b, h, i: (b, v_col0 + h)),
            pl.BlockSpec((SB_TQ, wq), lambda b, h, i: (b * nq + i, g_col0 + h)),
        ],
        out_specs=pl.BlockSpec((SB_TQ, wq), lambda b, h, i: (b * nq + i, h)),
        out_shape=jax.ShapeDtypeStruct((batch * seq, KV_A * wq), BF16),
        scratch_shapes=[
            pltpu.VMEM((G_A * SB_TQ, DH_A), F32),
            pltpu.VMEM((G_A * SB_TQ, 1), F32),
        ],
        compiler_params=_cparams(("parallel", "parallel", "arbitrary")),
        name="sb_prompt",
    )(pa, pa, pa, pa)


def _sb_sample_kernel(pt_ref, q_ref, kn_ref, vn_ref, *refs, n_new):
    del pt_ref
    npg = SB_PAGES_PER_STEP
    k_refs = refs[:npg]
    v_refs = refs[npg:2 * npg]
    o_ref = refs[2 * npg]
    acc_ref = refs[2 * npg + 1]
    carry_ref = refs[2 * npg + 2]
    c = pl.program_id(1)
    rows_h = G_A * n_new
    rows = KV_A * rows_h
    scale = DH_A ** -0.5 * LOG2E

    @pl.when(c == 0)
    def _():
        t_of_row = lax.broadcasted_iota(jnp.int32, (rows_h, 1), 0) & (n_new - 1)
        for h in range(KV_A):
            qh = q_ref[0, h] * scale
            kn = kn_ref[0, h]
            vn = vn_ref[0, h]
            ls_cols, l1m_cols = [], []
            for j in range(n_new):
                zj = jnp.sum(qh * kn[j:j + 1, :], axis=-1, keepdims=True)
                ls_j, l1m_j = _log2_beta_terms(zj)
                ls_cols.append(ls_j)
                l1m_cols.append(jnp.where(t_of_row > j, l1m_j, 0.0))
            acc = jnp.zeros((rows_h, DH_A), F32)
            after = jnp.zeros((rows_h, 1), F32)
            for j in reversed(range(n_new)):
                wj = jnp.where(t_of_row > j, jnp.exp2(ls_cols[j] + after), 0.0)
                acc = acc + wj * vn[j:j + 1, :]
                after = after + l1m_cols[j]
            acc_ref[h] = acc
            carry_ref[h * rows_h:(h + 1) * rows_h, :] = jnp.broadcast_to(after, (rows_h, PAGE))

    def head_rows(refs, h):
        return jnp.concatenate([r[pl.ds(h, PAGE, stride=KV_A), :] for r in refs], axis=0).astype(BF16)

    z = [_nt_dot((q_ref[0, h] * scale).astype(BF16), head_rows(k_refs, h)) for h in range(KV_A)]
    z = jnp.concatenate([z[h][:, i * PAGE:(i + 1) * PAGE] for i in range(npg) for h in range(KV_A)], axis=0)
    ls, l1m = _log2_beta_terms(z)
    suffix = jnp.concatenate([_suffix_matrix(PAGE), jnp.ones((PAGE, PAGE), BF16)], axis=1)
    sums = _dot(l1m.astype(BF16), suffix)
    after = sums[:, :PAGE]
    totals = sums[:, PAGE:]
    carry = carry_ref[...]
    ws = []
    for i in range(npg):
        sl = slice(i * rows, (i + 1) * rows)
        ws.append(jnp.exp2(ls[sl] + after[sl] + carry).astype(BF16))
        carry = carry + totals[sl]
    carry_ref[...] = carry
    for h in range(KV_A):
        w_h = jnp.concatenate([ws[i][h * rows_h:(h + 1) * rows_h] for i in range(npg)], axis=1)
        acc_ref[h] += _dot(w_h, head_rows(v_refs, h))

    @pl.when(c == pl.num_programs(1) - 1)
    def _():
        o_ref[0] = acc_ref[...]


def _sb_sample(q, k_new, v_new, cache_k, cache_v, page_table):
    db, _, rows_h, _ = q.shape
    ds = k_new.shape[2]
    n_pages = page_table.shape[1]
    npg = SB_PAGES_PER_STEP
    n_chunks = n_pages // npg

    def page_spec(i):
        return pl.BlockSpec((PAGE_ROWS, DH_A),
                            lambda r, c, pt: (pt[r, n_pages - 1 - (c * npg + i)], 0))

    in_specs = [
        pl.BlockSpec((1, KV_A, rows_h, DH_A), lambda r, c, pt: (r, 0, 0, 0)),
        pl.BlockSpec((1, KV_A, ds, DH_A), lambda r, c, pt: (r, 0, 0, 0)),
        pl.BlockSpec((1, KV_A, ds, DH_A), lambda r, c, pt: (r, 0, 0, 0)),
    ] + [page_spec(i) for i in range(npg)] + [page_spec(i) for i in range(npg)]
    grid_spec = pltpu.PrefetchScalarGridSpec(
        num_scalar_prefetch=1,
        grid=(db, n_chunks),
        in_specs=in_specs,
        out_specs=pl.BlockSpec((1, KV_A, rows_h, DH_A), lambda r, c, pt: (r, 0, 0, 0)),
        scratch_shapes=[
            pltpu.VMEM((KV_A, rows_h, DH_A), F32),
            pltpu.VMEM((KV_A * rows_h, PAGE), F32),
        ],
    )
    kern = functools.partial(_sb_sample_kernel, n_new=ds)
    return pl.pallas_call(
        kern,
        grid_spec=grid_spec,
        out_shape=jax.ShapeDtypeStruct((db, KV_A, rows_h, DH_A), F32),
        compiler_params=_cparams(("parallel", "arbitrary")),
        name="sb_sample",
    )(page_table, q, k_new, v_new, *([cache_k] * npg), *([cache_v] * npg))


def _sink_attention(qs, k, v, dist, valid, slope, sink):
    s = _nt_dot(qs, k)
    s = jnp.where(valid, s - slope * dist, NEG)
    m = jnp.maximum(jnp.max(s, axis=-1, keepdims=True), sink)
    p = jnp.exp(s - m)
    denom = jnp.sum(p, axis=-1, keepdims=True) + jnp.exp(sink - m)
    return _dot(p.astype(BF16), v) * (1.0 / denom)


def _head_columns(sinks_ref, kvh, rows_per_head):
    g_idx = lax.broadcasted_iota(jnp.int32, (G_B * rows_per_head, 1), 0) // rows_per_head
    slope = jnp.zeros((G_B * rows_per_head, 1), F32)
    sink = jnp.zeros((G_B * rows_per_head, 1), F32)
    for g in range(G_B):
        h = kvh * G_B + g
        slope = jnp.where(g_idx == g, 2.0 ** (-8.0 * (h + 1) / H_B), slope)
        sink = jnp.where(g_idx == g, sinks_ref[h], sink)
    return slope, sink


def _swa_prompt_kernel(sinks_ref, q_ref, g_ref, kvp_ref, kvc_ref, o_ref):
    n = pl.program_id(1)
    blk = WINDOW
    rows = G_B * blk
    q = q_ref[...] * (DH_B ** -0.5)
    gate = _silu(g_ref[...])
    kvp = kvp_ref[...]
    kvc = kvc_ref[...]
    i = lax.broadcasted_iota(jnp.int32, (rows, 2 * blk), 0) & (blk - 1)
    j = lax.broadcasted_iota(jnp.int32, (rows, 2 * blk), 1)
    dist = i - j + blk
    valid = (dist >= 0) & (dist < WINDOW) & ((n > 0) | (j >= blk))
    distf = dist.astype(F32)
    vo = KV_B * DH_B
    outs = []
    for kvh in range(KV_B):
        c0 = kvh * DH_B
        k = jnp.concatenate([kvp[:, c0:c0 + DH_B], kvc[:, c0:c0 + DH_B]], axis=0).astype(BF16)
        v = jnp.concatenate([kvp[:, vo + c0:vo + c0 + DH_B], kvc[:, vo + c0:vo + c0 + DH_B]], axis=0).astype(BF16)
        qs = jnp.concatenate(
            [q[:, (kvh * G_B + g) * DH_B:(kvh * G_B + g + 1) * DH_B] for g in range(G_B)], axis=0).astype(BF16)
        slope, sink = _head_columns(sinks_ref, kvh, blk)
        o = _sink_attention(qs, k, v, distf, valid, slope, sink)
        outs.extend(o[g * blk:(g + 1) * blk] for g in range(G_B))
    o_ref[...] = (jnp.concatenate(outs, axis=1) * gate).astype(BF16)


def _swa_prompt(pb, sinks, batch, seq):
    blk = WINDOW
    nb = seq // blk
    wq = H_B * DH_B
    wkv = 2 * KV_B * DH_B
    kv_col = (2 * wq) // wkv
    return pl.pallas_call(
        _swa_prompt_kernel,
        grid=(batch, nb),
        in_specs=[
            pl.BlockSpec(memory_space=pltpu.SMEM),
            pl.BlockSpec((blk, wq), lambda b, n: (b * nb + n, 0)),
            pl.BlockSpec((blk, wq), lambda b, n: (b * nb + n, 1)),
            pl.BlockSpec((blk, wkv), lambda b, n: (b * nb + jnp.maximum(n - 1, 0), kv_col)),
            pl.BlockSpec((blk, wkv), lambda b, n: (b * nb + n, kv_col)),
        ],
        out_specs=pl.BlockSpec((blk, wq), lambda b, n: (b * nb + n, 0)),
        out_shape=jax.ShapeDtypeStruct((batch * seq, wq), BF16),
        compiler_params=_cparams(("parallel", "arbitrary")),
        name="swa_prompt",
    )(sinks, pb, pb, pb, pb)


def _swa_sample_kernel(sinks_ref, q_ref, kc_ref, vc_ref, kn_ref, vn_ref, o_ref, *, n_new):
    n_buf = kc_ref.shape[1]
    rows = G_B * n_new
    keys = 2 * n_buf
    t = lax.broadcasted_iota(jnp.int32, (rows, keys), 0) & (n_new - 1)
    j = lax.broadcasted_iota(jnp.int32, (rows, keys), 1)
    dist = n_buf + t - j
    valid = (dist >= 0) & (dist < WINDOW)
    distf = dist.astype(F32)
    pad = jnp.zeros((n_buf - kn_ref.shape[1], DH_B), F32)
    for r in range(SWA_REQ_PER_STEP):
        kc = kc_ref[r]
        vc = vc_ref[r]
        kn = kn_ref[r]
        vn = vn_ref[r]
        for kvh in range(KV_B):
            c0 = kvh * DH_B
            k = jnp.concatenate([kc[:, c0:c0 + DH_B], kn[:, c0:c0 + DH_B], pad], axis=0).astype(BF16)
            v = jnp.concatenate([vc[:, c0:c0 + DH_B], vn[:, c0:c0 + DH_B], pad], axis=0).astype(BF16)
            qs = (q_ref[r, kvh] * (DH_B ** -0.5)).astype(BF16)
            slope, sink = _head_columns(sinks_ref, kvh, n_new)
            o_ref[r, kvh] = _sink_attention(qs, k, v, distf, valid, slope, sink)


def _swa_sample(q, cache_k, cache_v, k_new, v_new, sinks):
    db, _, rows, _ = q.shape
    n_buf = cache_k.shape[1]
    ds = k_new.shape[1]
    wkv = KV_B * DH_B
    rq = SWA_REQ_PER_STEP
    ds_pad = -(-ds // 8) * 8
    k_new = jnp.pad(k_new, ((0, 0), (0, ds_pad - ds), (0, 0)))
    v_new = jnp.pad(v_new, ((0, 0), (0, ds_pad - ds), (0, 0)))
    kern = functools.partial(_swa_sample_kernel, n_new=ds)
    return pl.pallas_call(
        kern,
        grid=(db // rq,),
        in_specs=[
            pl.BlockSpec(memory_space=pltpu.SMEM),
            pl.BlockSpec((rq, KV_B, rows, DH_B), lambda i: (i, 0, 0, 0)),
            pl.BlockSpec((rq, n_buf, wkv), lambda i: (i, 0, 0)),
            pl.BlockSpec((rq, n_buf, wkv), lambda i: (i, 0, 0)),
            pl.BlockSpec((rq, ds_pad, wkv), lambda i: (i, 0, 0)),
            pl.BlockSpec((rq, ds_pad, wkv), lambda i: (i, 0, 0)),
        ],
        out_specs=pl.BlockSpec((rq, KV_B, rows, DH_B), lambda i: (i, 0, 0, 0)),
        out_shape=jax.ShapeDtypeStruct((db, KV_B, rows, DH_B), F32),
        compiler_params=_cparams(("parallel",)),
        name="swa_sample",
    )(sinks, q, cache_k, cache_v, k_new, v_new)


def kernel(x_prompt, x_sample, cache_a_k, cache_a_v, cache_b_k, cache_b_v, page_table,
           norm_a, w_in_a, w_out_a, norm_kv, w_kv, norm_b, w_in_b, sinks_b, w_out_b, norm_f):
    batch, seq, d = x_prompt.shape
    db, ds, _ = x_sample.shape
    w_buf = cache_b_k.shape[1]
    n_pool, page = cache_a_k.shape[1], cache_a_k.shape[2]
    wq_a = KV_A * G_A * DH_A
    wkv_a = KV_A * DH_A
    wq_b = H_B * DH_B
    wkv_b = KV_B * DH_B

    w_in_a0 = w_in_a[0].astype(BF16)
    w_out_a0 = w_out_a[0].astype(BF16)
    w_in_bkv = jnp.concatenate([w_in_b[0], w_kv], axis=1).astype(BF16)
    w_out_b0 = w_out_b[0].astype(BF16)
    gains_b = jnp.stack([norm_b[0], norm_kv], axis=0)
    nf = norm_f.reshape(1, d)
    sinks = sinks_b[0]
    split_b = (2 * wq_b) // 512

    xp = x_prompt.reshape(batch * seq, d)
    pa = _inproj(xp, norm_a[0:1], w_in_a0, tm=1024, tn=1024)
    a_k_prompt = pa[:, wq_a:wq_a + wkv_a].reshape(1, batch, seq, KV_A, DH_A)
    a_v_prompt = pa[:, wq_a + wkv_a:wq_a + 2 * wkv_a].reshape(1, batch, seq, KV_A, DH_A)
    og = _sb_prompt(pa, batch, seq)
    h1 = _outproj(xp, og, None, w_out_a0, None, tm=512)
    pb = _inproj(h1, gains_b, w_in_bkv, tm=1024, tn=512, split=split_b)
    kvb = pb[:, 2 * wq_b:].reshape(batch, seq, 2 * wkv_b)
    b_k_prompt = kvb[:, seq - w_buf:, :wkv_b].reshape(batch, w_buf, KV_B, DH_B)
    b_v_prompt = kvb[:, seq - w_buf:, wkv_b:].reshape(batch, w_buf, KV_B, DH_B)
    og2 = _swa_prompt(pb, sinks, batch, seq)
    y_prompt = _outproj(h1, og2, None, w_out_b0, nf, tm=512).reshape(batch, seq, d)

    ts = db * ds
    xs = x_sample.reshape(ts, d)
    pas = _inproj(xs, norm_a[0:1], w_in_a0, tm=ts, tn=512)
    k_new_a = pas[:, wq_a:wq_a + wkv_a]
    v_new_a = pas[:, wq_a + wkv_a:wq_a + 2 * wkv_a]
    a_k_sample = k_new_a.reshape(1, db, ds, KV_A, DH_A)
    a_v_sample = v_new_a.reshape(1, db, ds, KV_A, DH_A)
    q_s = pas[:, :wq_a].reshape(db, ds, KV_A, G_A, DH_A).transpose(0, 2, 3, 1, 4).reshape(db, KV_A, G_A * ds, DH_A)
    kn_s = k_new_a.reshape(db, ds, KV_A, DH_A).transpose(0, 2, 1, 3)
    vn_s = v_new_a.reshape(db, ds, KV_A, DH_A).transpose(0, 2, 1, 3)
    o_s = _sb_sample(q_s, kn_s, vn_s,
                     cache_a_k[0].reshape(n_pool * page * KV_A, DH_A),
                     cache_a_v[0].reshape(n_pool * page * KV_A, DH_A),
                     page_table)
    o_s = o_s.reshape(db, KV_A, G_A, ds, DH_A).transpose(0, 3, 1, 2, 4).reshape(ts, wq_a)
    h1s = _outproj(xs, o_s, pas[:, wq_a + 2 * wkv_a:], w_out_a0, None, tm=ts)
    pbs = _inproj(h1s, gains_b, w_in_bkv, tm=ts, tn=512, split=split_b)
    k_new_b = pbs[:, 2 * wq_b:2 * wq_b + wkv_b].reshape(db, ds, wkv_b)
    v_new_b = pbs[:, 2 * wq_b + wkv_b:].reshape(db, ds, wkv_b)
    q_bs = pbs[:, :wq_b].reshape(db, ds, KV_B, G_B, DH_B).transpose(0, 2, 3, 1, 4).reshape(db, KV_B, G_B * ds, DH_B)
    o_bs = _swa_sample(q_bs, cache_b_k.reshape(db, w_buf, wkv_b), cache_b_v.reshape(db, w_buf, wkv_b),
                       k_new_b, v_new_b, sinks)
    o_bs = o_bs.reshape(db, KV_B, G_B, ds, DH_B).transpose(0, 3, 1, 2, 4).reshape(ts, wq_b)
    y_sample = _outproj(h1s, o_bs, pbs, w_out_b0, nf, tm=ts, gate_col_block=1).reshape(db, ds, d)

    kb_all = jnp.concatenate([cache_b_k, k_new_b.reshape(db, ds, KV_B, DH_B)], axis=1)
    vb_all = jnp.concatenate([cache_b_v, v_new_b.reshape(db, ds, KV_B, DH_B)], axis=1)
    b_k_sample = kb_all[:, -w_buf:]
    b_v_sample = vb_all[:, -w_buf:]

    return (y_prompt, y_sample, a_k_prompt, a_v_prompt, a_k_sample, a_v_sample,
            b_k_prompt, b_v_prompt, b_k_sample, b_v_sample)
```

```python
import functools

import jax
import jax.numpy as jnp
from jax import lax
from jax.experimental import pallas as pl
from jax.experimental.pallas import tpu as pltpu

F32 = jnp.float32
BF16 = jnp.bfloat16

D_MODEL = 2048
DH_A = 128
KV_A = 4
G_A = 4
DH_B = 64
KV_B = 4
G_B = 8
H_B = KV_B * G_B
WINDOW = 128
EPS = 1e-6
NEG = -1e30

VMEM_LIMIT_BYTES = 56 * 1024 * 1024

SB_TQ = 256
SB_TK = 256
SB_PAGES_PER_STEP = 16
PAGE = 128
PAGE_ROWS = PAGE * KV_A
SWA_REQ_PER_STEP = 8


def _cparams(sem):
    return pltpu.CompilerParams(dimension_semantics=sem, vmem_limit_bytes=VMEM_LIMIT_BYTES)


def _nt_dot(a, b):
    return lax.dot_general(a, b, (((1,), (1,)), ((), ())), preferred_element_type=F32)


def _dot(a, b):
    return jnp.dot(a, b, preferred_element_type=F32)


def _silu(g):
    return g * (1.0 / (1.0 + jnp.exp(-g)))


def _suffix_matrix(n):
    row = lax.broadcasted_iota(jnp.int32, (n, n), 0)
    col = lax.broadcasted_iota(jnp.int32, (n, n), 1)
    return jnp.where(row > col, 1.0, 0.0).astype(BF16)


LOG2E = 1.4426950408889634


def _log2_beta_terms(z2):
    lo = jnp.minimum(z2, 0.0)
    hi = lo - z2
    l = jnp.log2(1.0 + jnp.exp2(lo + hi))
    return lo - l, hi - l


def _inproj_kernel(x_ref, g_ref, w_ref, *refs, n_gains, split, kv_rows):
    o_ref, xn_ref = refs[0], refs[-1]
    j = pl.program_id(1)
    tm = x_ref.shape[0]

    @pl.when(j == 0)
    def _():
        x = x_ref[...]
        xh = x * lax.rsqrt(jnp.mean(x * x, axis=-1, keepdims=True) + EPS)
        for gi in range(n_gains):
            xn_ref[gi] = (xh * g_ref[gi:gi + 1, :]).astype(BF16)

    if n_gains == 1:
        res = _dot(xn_ref[0], w_ref[...])
        o_ref[...] = res
        if kv_rows:
            k_rows_ref, v_rows_ref = refs[1], refs[2]
            wkv = KV_A * DH_A

            @pl.when(j == 0)
            def _():
                for h in range(KV_A):
                    k_rows_ref[pl.ds(h, tm, stride=KV_A), :] = res[:, h * DH_A:(h + 1) * DH_A]
                    v_rows_ref[pl.ds(h, tm, stride=KV_A), :] = res[:, wkv + h * DH_A:wkv + (h + 1) * DH_A]
    else:
        @pl.when(j < split)
        def _():
            o_ref[...] = _dot(xn_ref[0], w_ref[...])

        @pl.when(j >= split)
        def _():
            o_ref[...] = _dot(xn_ref[1], w_ref[...])


def _inproj(x, gains, w, *, tm, tn, split=0, kv_rows=False):
    t, d = x.shape
    n = w.shape[1]
    n_gains = gains.shape[0]
    kern = functools.partial(_inproj_kernel, n_gains=n_gains, split=split, kv_rows=kv_rows)
    out_specs = [pl.BlockSpec((tm, tn), lambda i, j: (i, j))]
    out_shape = [jax.ShapeDtypeStruct((t, n), F32)]
    if kv_rows:
        assert n_gains == 1 and tn == 2 * KV_A * DH_A
        for _ in range(2):
            out_specs.append(pl.BlockSpec((tm * KV_A, DH_A), lambda i, j: (i, 0)))
            out_shape.append(jax.ShapeDtypeStruct((t * KV_A, DH_A), F32))
    return pl.pallas_call(
        kern,
        grid=(t // tm, n // tn),
        in_specs=[
            pl.BlockSpec((tm, d), lambda i, j: (i, 0)),
            pl.BlockSpec((n_gains, d), lambda i, j: (0, 0)),
            pl.BlockSpec((d, tn), lambda i, j: (0, j)),
        ],
        out_specs=out_specs,
        out_shape=out_shape,
        scratch_shapes=[pltpu.VMEM((n_gains, tm, d), BF16)],
        compiler_params=_cparams(("parallel", "arbitrary")),
        name="inproj",
    )(x, gains, w)


def _outproj_kernel(*refs, gated, final_norm):
    refs = list(refs)
    h_ref = refs.pop(0)
    o_ref = refs.pop(0)
    g_ref = refs.pop(0) if gated else None
    w_ref = refs.pop(0)
    nf_ref = refs.pop(0) if final_norm else None
    out_ref = refs.pop(0)

    if gated:
        og = (o_ref[...] * _silu(g_ref[...])).astype(BF16)
    else:
        og = o_ref[...]
    h = h_ref[...] + _dot(og, w_ref[...])
    if final_norm:
        h = h * lax.rsqrt(jnp.mean(h * h, axis=-1, keepdims=True) + EPS) * nf_ref[...]
    out_ref[...] = h


def _outproj(h, o, gate, w, norm_f, *, tm, gate_col_block=0):
    t, d = h.shape
    gated = gate is not None
    final_norm = norm_f is not None
    args = [h, o]
    in_specs = [
        pl.BlockSpec((tm, d), lambda i: (i, 0)),
        pl.BlockSpec((tm, d), lambda i: (i, 0)),
    ]
    if gated:
        args.append(gate)
        in_specs.append(pl.BlockSpec((tm, d), lambda i: (i, gate_col_block)))
    args.append(w)
    in_specs.append(pl.BlockSpec((d, d), lambda i: (0, 0)))
    if final_norm:
        args.append(norm_f)
        in_specs.append(pl.BlockSpec((1, d), lambda i: (0, 0)))
    kern = functools.partial(_outproj_kernel, gated=gated, final_norm=final_norm)
    return pl.pallas_call(
        kern,
        grid=(t // tm,),
        in_specs=in_specs,
        out_specs=pl.BlockSpec((tm, d), lambda i: (i, 0)),
        out_shape=jax.ShapeDtypeStruct((t, d), F32),
        compiler_params=_cparams(("parallel",)),
        name="outproj",
    )(*args)


def _sb_prompt_kernel(q_ref, k_ref, v_ref, g_ref, o_ref, acc_ref, carry_ref):
    qi = pl.program_id(2)
    q = q_ref[...] * (DH_A ** -0.5 * LOG2E)
    qs = jnp.concatenate([q[:, g * DH_A:(g + 1) * DH_A] for g in range(G_A)], axis=0).astype(BF16)
    suffix = _suffix_matrix(SB_TK)
    rows = G_A * SB_TQ

    def tiles(kb_new, n, masked):
        start = pl.multiple_of((kb_new - (n - 1)) * SB_TK, SB_TK)
        k = k_ref[pl.ds(start, n * SB_TK), :].astype(BF16)
        v = v_ref[pl.ds(start, n * SB_TK), :].astype(BF16)
        z = _nt_dot(qs, k)
        order = list(reversed(range(n)))
        ls, l1m, after = {}, {}, {}
        valid = None
        for t in order:
            ls[t], l1m[t] = _log2_beta_terms(z[:, t * SB_TK:(t + 1) * SB_TK])
            if masked and t == n - 1:
                qpos = lax.broadcasted_iota(jnp.int32, (rows, SB_TK), 0) & (SB_TQ - 1)
                kpos = lax.broadcasted_iota(jnp.int32, (rows, SB_TK), 1)
                valid = kpos < qpos
                l1m[t] = jnp.where(valid, l1m[t], 0.0)
            after[t] = _dot(l1m[t].astype(BF16), suffix)
        carry = carry_ref[...]
        w = [None] * n
        for t in order:
            wt = jnp.exp2(ls[t] + after[t] + carry)
            if masked and t == n - 1:
                wt = jnp.where(valid, wt, 0.0)
            w[t] = wt.astype(BF16)
            carry = carry + (after[t][:, 0:1] + l1m[t][:, 0:1])
        carry_ref[...] = carry
        acc_ref[...] += _dot(jnp.concatenate(w, axis=1) if n > 1 else w[0], v)

    acc_ref[...] = jnp.zeros_like(acc_ref)
    carry_ref[...] = jnp.zeros_like(carry_ref)
    odd = qi & 1

    @pl.when(odd == 0)
    def _():
        tiles(qi, 1, True)

    @pl.when(odd == 1)
    def _():
        tiles(qi, 2, True)

    first = qi - 1 - odd

    def body(it, c):
        tiles(first - 2 * it, 2, False)
        return c

    lax.fori_loop(0, lax.shift_right_logical(qi, 1), body, 0)

    gate = _silu(g_ref[...])
    acc = acc_ref[...]
    o_ref[...] = (jnp.concatenate(
        [acc[g * SB_TQ:(g + 1) * SB_TQ] for g in range(G_A)], axis=1) * gate).astype(BF16)


def _sb_prompt(pa, batch, seq):
    nq = seq // SB_TQ
    wq = G_A * DH_A
    k_col0 = 0
    v_col0 = KV_A
    q_col0 = (2 * KV_A * DH_A) // wq
    g_col0 = q_col0 + KV_A
    return pl.pallas_call(
        _sb_prompt_kernel,
        grid=(batch, KV_A, nq),
        in_specs=[
            pl.BlockSpec((SB_TQ, wq), lambda b, h, i: (b * nq + i, q_col0 + h)),
            pl.BlockSpec((seq, DH_A), lambda b, h, i: (b, k_col0 + h)),
            pl.BlockSpec((seq, DH_A), lambda b, h, i: (b, v_col0 + h)),
            pl.BlockSpec((SB_TQ, wq), lambda b, h, i: (b * nq + i, g_col0 + h)),
        ],
        out_specs=pl.BlockSpec((SB_TQ, wq), lambda b, h, i: (b * nq + i, h)),
        out_shape=jax.ShapeDtypeStruct((batch * seq, KV_A * wq), BF16),
        scratch_shapes=[
            pltpu.VMEM((G_A * SB_TQ, DH_A), F32),
            pltpu.VMEM((G_A * SB_TQ, 1), F32),
        ],
        compiler_params=_cparams(("parallel", "parallel", "arbitrary")),
        name="sb_prompt",
    )(pa, pa, pa, pa)


def _sb_sample_kernel(pt_ref, q_ref, kn_ref, vn_ref, *refs, n_new):
    del pt_ref
    npg = SB_PAGES_PER_STEP
    k_refs = refs[:npg]
    v_refs = refs[npg:2 * npg]
    o_ref = refs[2 * npg]
    acc_ref = refs[2 * npg + 1]
    carry_ref = refs[2 * npg + 2]
    c = pl.program_id(1)
    rows_h = G_A * n_new
    rows = KV_A * rows_h
    scale = DH_A ** -0.5 * LOG2E

    @pl.when(c == 0)
    def _():
        t_of_row = lax.broadcasted_iota(jnp.int32, (rows_h, 1), 0) & (n_new - 1)
        for h in range(KV_A):
            qh = q_ref[0, h] * scale
            kn = kn_ref[0, h]
            vn = vn_ref[0, h]
            ls_cols, l1m_cols = [], []
            for j in range(n_new):
                zj = jnp.sum(qh * kn[j:j + 1, :], axis=-1, keepdims=True)
                ls_j, l1m_j = _log2_beta_terms(zj)
                ls_cols.append(ls_j)
                l1m_cols.append(jnp.where(t_of_row > j, l1m_j, 0.0))
            acc = jnp.zeros((rows_h, DH_A), F32)
            after = jnp.zeros((rows_h, 1), F32)
            for j in reversed(range(n_new)):
                wj = jnp.where(t_of_row > j, jnp.exp2(ls_cols[j] + after), 0.0)
                acc = acc + wj * vn[j:j + 1, :]
                after = after + l1m_cols[j]
            acc_ref[h] = acc
            carry_ref[h * rows_h:(h + 1) * rows_h, :] = jnp.broadcast_to(after, (rows_h, PAGE))

    def head_rows(refs, h):
        return jnp.concatenate([r[pl.ds(h, PAGE, stride=KV_A), :] for r in refs], axis=0).astype(BF16)

    z = [_nt_dot((q_ref[0, h] * scale).astype(BF16), head_rows(k_refs, h)) for h in range(KV_A)]
    z = jnp.concatenate([z[h][:, i * PAGE:(i + 1) * PAGE] for i in range(npg) for h in range(KV_A)], axis=0)
    ls, l1m = _log2_beta_terms(z)
    suffix = jnp.concatenate([_suffix_matrix(PAGE), jnp.ones((PAGE, PAGE), BF16)], axis=1)
    sums = _dot(l1m.astype(BF16), suffix)
    after = sums[:, :PAGE]
    totals = sums[:, PAGE:]
    carry = carry_ref[...]
    ws = []
    for i in range(npg):
        sl = slice(i * rows, (i + 1) * rows)
        ws.append(jnp.exp2(ls[sl] + after[sl] + carry).astype(BF16))
        carry = carry + totals[sl]
    carry_ref[...] = carry
    for h in range(KV_A):
        w_h = jnp.concatenate([ws[i][h * rows_h:(h + 1) * rows_h] for i in range(npg)], axis=1)
        acc_ref[h] += _dot(w_h, head_rows(v_refs, h))

    @pl.when(c == pl.num_programs(1) - 1)
    def _():
        o_ref[0] = acc_ref[...]


def _sb_sample(q, k_new, v_new, cache_k, cache_v, page_table):
    db, _, rows_h, _ = q.shape
    ds = k_new.shape[2]
    n_pages = page_table.shape[1]
    npg = SB_PAGES_PER_STEP
    n_chunks = n_pages // npg

    def page_spec(i):
        return pl.BlockSpec((PAGE_ROWS, DH_A),
                            lambda r, c, pt: (pt[r, n_pages - 1 - (c * npg + i)], 0))

    in_specs = [
        pl.BlockSpec((1, KV_A, rows_h, DH_A), lambda r, c, pt: (r, 0, 0, 0)),
        pl.BlockSpec((1, KV_A, ds, DH_A), lambda r, c, pt: (r, 0, 0, 0)),
        pl.BlockSpec((1, KV_A, ds, DH_A), lambda r, c, pt: (r, 0, 0, 0)),
    ] + [page_spec(i) for i in range(npg)] + [page_spec(i) for i in range(npg)]
    grid_spec = pltpu.PrefetchScalarGridSpec(
        num_scalar_prefetch=1,
        grid=(db, n_chunks),
        in_specs=in_specs,
        out_specs=pl.BlockSpec((1, KV_A, rows_h, DH_A), lambda r, c, pt: (r, 0, 0, 0)),
        scratch_shapes=[
            pltpu.VMEM((KV_A, rows_h, DH_A), F32),
            pltpu.VMEM((KV_A * rows_h, PAGE), F32),
        ],
    )
    kern = functools.partial(_sb_sample_kernel, n_new=ds)
    return pl.pallas_call(
        kern,
        grid_spec=grid_spec,
        out_shape=jax.ShapeDtypeStruct((db, KV_A, rows_h, DH_A), F32),
        compiler_params=_cparams(("parallel", "arbitrary")),
        name="sb_sample",
    )(page_table, q, k_new, v_new, *([cache_k] * npg), *([cache_v] * npg))


def _sink_attention(qs, k, v, dist, valid, slope, sink):
    s = _nt_dot(qs, k)
    s = jnp.where(valid, s - slope * dist, NEG)
    m = jnp.maximum(jnp.max(s, axis=-1, keepdims=True), sink)
    p = jnp.exp(s - m)
    denom = jnp.sum(p, axis=-1, keepdims=True) + jnp.exp(sink - m)
    return _dot(p.astype(BF16), v) * (1.0 / denom)


def _head_columns(sinks_ref, kvh, rows_per_head):
    g_idx = lax.broadcasted_iota(jnp.int32, (G_B * rows_per_head, 1), 0) // rows_per_head
    slope = jnp.zeros((G_B * rows_per_head, 1), F32)
    sink = jnp.zeros((G_B * rows_per_head, 1), F32)
    for g in range(G_B):
        h = kvh * G_B + g
        slope = jnp.where(g_idx == g, 2.0 ** (-8.0 * (h + 1) / H_B), slope)
        sink = jnp.where(g_idx == g, sinks_ref[h], sink)
    return slope, sink


def _swa_prompt_kernel(sinks_ref, q_ref, g_ref, kvp_ref, kvc_ref, bias_ref, o_ref):
    n = pl.program_id(1)
    blk = WINDOW
    q = q_ref[...] * (DH_B ** -0.5 * LOG2E)
    kvp = kvp_ref[...]
    kvc = kvc_ref[...]
    no_prev = jnp.where(n > 0, 0.0, NEG)
    vo = KV_B * DH_B
    outs = []
    for kvh in range(KV_B):
        c0 = kvh * DH_B
        k = jnp.concatenate([kvp[:, c0:c0 + DH_B], kvc[:, c0:c0 + DH_B]], axis=0).astype(BF16)
        v = jnp.concatenate([kvp[:, vo + c0:vo + c0 + DH_B], kvc[:, vo + c0:vo + c0 + DH_B]], axis=0).astype(BF16)
        qs = jnp.concatenate(
            [q[:, (kvh * G_B + g) * DH_B:(kvh * G_B + g + 1) * DH_B] for g in range(G_B)], axis=0).astype(BF16)
        z = _nt_dot(qs, k)
        ps, inv = [], []
        for g in range(G_B):
            h = kvh * G_B + g
            s = z[g * blk:(g + 1) * blk] + bias_ref[h]
            s = jnp.concatenate([s[:, :blk] + no_prev, s[:, blk:]], axis=1)
            sink = sinks_ref[h] * LOG2E
            m = jnp.maximum(jnp.max(s, axis=-1, keepdims=True), sink)
            p = jnp.exp2(s - m)
            inv.append(1.0 / (jnp.sum(p, axis=-1, keepdims=True) + jnp.exp2(sink - m)))
            ps.append(p.astype(BF16))
        o = _dot(jnp.concatenate(ps, axis=0), v)
        outs.extend(o[g * blk:(g + 1) * blk] * inv[g] for g in range(G_B))
    o_ref[...] = (jnp.concatenate(outs, axis=1) * _silu(g_ref[...])).astype(BF16)


def _swa_bias():
    i = jnp.arange(WINDOW)[:, None]
    j = jnp.arange(2 * WINDOW)[None, :]
    dist = i - j + WINDOW
    band = (dist >= 0) & (dist < WINDOW)
    slopes = jnp.exp2(-8.0 * jnp.arange(1, H_B + 1, dtype=F32) / H_B)
    return jnp.where(band[None], -slopes[:, None, None] * dist[None].astype(F32) * LOG2E, NEG)


def _swa_prompt(pb, sinks, batch, seq):
    blk = WINDOW
    nb = seq // blk
    wq = H_B * DH_B
    wkv = 2 * KV_B * DH_B
    kv_col = (2 * wq) // wkv
    return pl.pallas_call(
        _swa_prompt_kernel,
        grid=(batch, nb),
        in_specs=[
            pl.BlockSpec(memory_space=pltpu.SMEM),
            pl.BlockSpec((blk, wq), lambda b, n: (b * nb + n, 0)),
            pl.BlockSpec((blk, wq), lambda b, n: (b * nb + n, 1)),
            pl.BlockSpec((blk, wkv), lambda b, n: (b * nb + jnp.maximum(n - 1, 0), kv_col)),
            pl.BlockSpec((blk, wkv), lambda b, n: (b * nb + n, kv_col)),
            pl.BlockSpec((H_B, blk, 2 * blk), lambda b, n: (0, 0, 0)),
        ],
        out_specs=pl.BlockSpec((blk, wq), lambda b, n: (b * nb + n, 0)),
        out_shape=jax.ShapeDtypeStruct((batch * seq, wq), BF16),
        compiler_params=_cparams(("parallel", "arbitrary")),
        name="swa_prompt",
    )(sinks, pb, pb, pb, pb, _swa_bias())


def _swa_sample_kernel(sinks_ref, q_ref, kc_ref, vc_ref, kn_ref, vn_ref, o_ref, *, n_new):
    n_buf = kc_ref.shape[1]
    rows = G_B * n_new
    keys = 2 * n_buf
    t = lax.broadcasted_iota(jnp.int32, (rows, keys), 0) & (n_new - 1)
    j = lax.broadcasted_iota(jnp.int32, (rows, keys), 1)
    dist = n_buf + t - j
    valid = (dist >= 0) & (dist < WINDOW)
    distf = dist.astype(F32)
    pad = jnp.zeros((n_buf - kn_ref.shape[1], DH_B), F32)
    for r in range(SWA_REQ_PER_STEP):
        kc = kc_ref[r]
        vc = vc_ref[r]
        kn = kn_ref[r]
        vn = vn_ref[r]
        for kvh in range(KV_B):
            c0 = kvh * DH_B
            k = jnp.concatenate([kc[:, c0:c0 + DH_B], kn[:, c0:c0 + DH_B], pad], axis=0).astype(BF16)
            v = jnp.concatenate([vc[:, c0:c0 + DH_B], vn[:, c0:c0 + DH_B], pad], axis=0).astype(BF16)
            qs = (q_ref[r, kvh] * (DH_B ** -0.5)).astype(BF16)
            slope, sink = _head_columns(sinks_ref, kvh, n_new)
            o_ref[r, kvh] = _sink_attention(qs, k, v, distf, valid, slope, sink)


def _swa_sample(q, cache_k, cache_v, k_new, v_new, sinks):
    db, _, rows, _ = q.shape
    n_buf = cache_k.shape[1]
    ds = k_new.shape[1]
    wkv = KV_B * DH_B
    rq = SWA_REQ_PER_STEP
    ds_pad = -(-ds // 8) * 8
    k_new = jnp.pad(k_new, ((0, 0), (0, ds_pad - ds), (0, 0)))
    v_new = jnp.pad(v_new, ((0, 0), (0, ds_pad - ds), (0, 0)))
    kern = functools.partial(_swa_sample_kernel, n_new=ds)
    return pl.pallas_call(
        kern,
        grid=(db // rq,),
        in_specs=[
            pl.BlockSpec(memory_space=pltpu.SMEM),
            pl.BlockSpec((rq, KV_B, rows, DH_B), lambda i: (i, 0, 0, 0)),
            pl.BlockSpec((rq, n_buf, wkv), lambda i: (i, 0, 0)),
            pl.BlockSpec((rq, n_buf, wkv), lambda i: (i, 0, 0)),
            pl.BlockSpec((rq, ds_pad, wkv), lambda i: (i, 0, 0)),
            pl.BlockSpec((rq, ds_pad, wkv), lambda i: (i, 0, 0)),
        ],
        out_specs=pl.BlockSpec((rq, KV_B, rows, DH_B), lambda i: (i, 0, 0, 0)),
        out_shape=jax.ShapeDtypeStruct((db, KV_B, rows, DH_B), F32),
        compiler_params=_cparams(("parallel",)),
        name="swa_sample",
    )(sinks, q, cache_k, cache_v, k_new, v_new)


def kernel(x_prompt, x_sample, cache_a_k, cache_a_v, cache_b_k, cache_b_v, page_table,
           norm_a, w_in_a, w_out_a, norm_kv, w_kv, norm_b, w_in_b, sinks_b, w_out_b, norm_f):
    batch, seq, d = x_prompt.shape
    db, ds, _ = x_sample.shape
    w_buf = cache_b_k.shape[1]
    n_pool, page = cache_a_k.shape[1], cache_a_k.shape[2]
    wq_a = KV_A * G_A * DH_A
    wkv_a = KV_A * DH_A
    wq_b = H_B * DH_B
    wkv_b = KV_B * DH_B

    w_a = w_in_a[0]
    w_in_a0 = jnp.concatenate([w_a[:, wq_a:wq_a + 2 * wkv_a], w_a[:, :wq_a], w_a[:, wq_a + 2 * wkv_a:]],
                              axis=1).astype(BF16)
    q_lo, g_lo = 2 * wkv_a, 2 * wkv_a + wq_a
    w_out_a0 = w_out_a[0].astype(BF16)
    w_in_bkv = jnp.concatenate([w_in_b[0], w_kv], axis=1).astype(BF16)
    w_out_b0 = w_out_b[0].astype(BF16)
    gains_b = jnp.stack([norm_b[0], norm_kv], axis=0)
    nf = norm_f.reshape(1, d)
    sinks = sinks_b[0]
    split_b = (2 * wq_b) // 512

    xp = x_prompt.reshape(batch * seq, d)
    pa, k_rows, v_rows = _inproj(xp, norm_a[0:1], w_in_a0, tm=1024, tn=2 * wkv_a, kv_rows=True)
    a_k_prompt = k_rows.reshape(1, batch, seq, KV_A, DH_A)
    a_v_prompt = v_rows.reshape(1, batch, seq, KV_A, DH_A)
    og = _sb_prompt(pa, batch, seq)
    h1 = _outproj(xp, og, None, w_out_a0, None, tm=512)
    pb = _inproj(h1, gains_b, w_in_bkv, tm=1024, tn=512, split=split_b)[0]
    kvb = pb.reshape(batch, seq, 2 * wq_b + 2 * wkv_b)[:, seq - w_buf:, 2 * wq_b:]
    b_k_prompt = kvb[:, :, :wkv_b].reshape(batch, w_buf, KV_B, DH_B)
    b_v_prompt = kvb[:, :, wkv_b:].reshape(batch, w_buf, KV_B, DH_B)
    og2 = _swa_prompt(pb, sinks, batch, seq)
    y_prompt = _outproj(h1, og2, None, w_out_b0, nf, tm=512).reshape(batch, seq, d)

    ts = db * ds
    xs = x_sample.reshape(ts, d)
    pas, k_rows_s, v_rows_s = _inproj(xs, norm_a[0:1], w_in_a0, tm=ts, tn=2 * wkv_a, kv_rows=True)
    a_k_sample = k_rows_s.reshape(1, db, ds, KV_A, DH_A)
    a_v_sample = v_rows_s.reshape(1, db, ds, KV_A, DH_A)
    q_s = pas[:, q_lo:g_lo].reshape(db, ds, KV_A, G_A, DH_A).transpose(0, 2, 3, 1, 4).reshape(
        db, KV_A, G_A * ds, DH_A)
    kn_s = k_rows_s.reshape(db, ds, KV_A, DH_A).transpose(0, 2, 1, 3)
    vn_s = v_rows_s.reshape(db, ds, KV_A, DH_A).transpose(0, 2, 1, 3)
    o_s = _sb_sample(q_s, kn_s, vn_s,
                     cache_a_k[0].reshape(n_pool * page * KV_A, DH_A),
                     cache_a_v[0].reshape(n_pool * page * KV_A, DH_A),
                     page_table)
    o_s = o_s.reshape(db, KV_A, G_A, ds, DH_A).transpose(0, 3, 1, 2, 4).reshape(ts, wq_a)
    h1s = _outproj(xs, o_s, pas[:, g_lo:], w_out_a0, None, tm=ts)
    pbs = _inproj(h1s, gains_b, w_in_bkv, tm=ts, tn=512, split=split_b)[0]
    k_new_b = pbs[:, 2 * wq_b:2 * wq_b + wkv_b].reshape(db, ds, wkv_b)
    v_new_b = pbs[:, 2 * wq_b + wkv_b:].reshape(db, ds, wkv_b)
    q_bs = pbs[:, :wq_b].reshape(db, ds, KV_B, G_B, DH_B).transpose(0, 2, 3, 1, 4).reshape(db, KV_B, G_B * ds, DH_B)
    o_bs = _swa_sample(q_bs, cache_b_k.reshape(db, w_buf, wkv_b), cache_b_v.reshape(db, w_buf, wkv_b),
                       k_new_b, v_new_b, sinks)
    o_bs = o_bs.reshape(db, KV_B, G_B, ds, DH_B).transpose(0, 3, 1, 2, 4).reshape(ts, wq_b)
    y_sample = _outproj(h1s, o_bs, pbs, w_out_b0, nf, tm=ts, gate_col_block=1).reshape(db, ds, d)

    kb_all = jnp.concatenate([cache_b_k, k_new_b.reshape(db, ds, KV_B, DH_B)], axis=1)
    vb_all = jnp.concatenate([cache_b_v, v_new_b.reshape(db, ds, KV_B, DH_B)], axis=1)
    b_k_sample = kb_all[:, -w_buf:]
    b_v_sample = vb_all[:, -w_buf:]

    return (y_prompt, y_sample, a_k_prompt, a_v_prompt, a_k_sample, a_v_sample,
            b_k_prompt, b_v_prompt, b_k_sample, b_v_sample)
```

```python
import functools

import jax
import jax.numpy as jnp
from jax import lax
from jax.experimental import pallas as pl
from jax.experimental.pallas import tpu as pltpu

F32 = jnp.float32
BF16 = jnp.bfloat16

D_MODEL = 2048
DH_A = 128
KV_A = 4
G_A = 4
DH_B = 64
KV_B = 4
G_B = 8
H_B = KV_B * G_B
WINDOW = 128
EPS = 1e-6
NEG = -1e30

VMEM_LIMIT_BYTES = 56 * 1024 * 1024

SB_TQ = 256
SB_TK = 256
SB_PAGES_PER_STEP = 16
SB_RING = 3
PAGE = 128
PAGE_ROWS = PAGE * KV_A
SWA_REQ_PER_STEP = 8
NORM_ROWS = 128
SWA_QBLOCKS = 1


def _cparams(sem):
    return pltpu.CompilerParams(dimension_semantics=sem, vmem_limit_bytes=VMEM_LIMIT_BYTES)


def _nt_dot(a, b):
    return lax.dot_general(a, b, (((1,), (1,)), ((), ())), preferred_element_type=F32)


def _dot(a, b):
    return jnp.dot(a, b, preferred_element_type=F32)


def _silu(g):
    return g * (1.0 / (1.0 + jnp.exp(-g)))


def _suffix_matrix(n):
    row = lax.broadcasted_iota(jnp.int32, (n, n), 0)
    col = lax.broadcasted_iota(jnp.int32, (n, n), 1)
    return jnp.where(row > col, 1.0, 0.0).astype(BF16)


LOG2E = 1.4426950408889634


def _log2_beta_terms(z2):
    lo = jnp.minimum(z2, 0.0)
    hi = lo - z2
    l = jnp.log2(1.0 + jnp.exp2(lo + hi))
    return lo - l, hi - l


def _inproj_kernel(x_ref, g_ref, w_ref, *refs, n_gains, split, kv_rows):
    o_ref, xn_ref = refs[0], refs[-1]
    j = pl.program_id(1)
    tm = x_ref.shape[0]

    @pl.when(j == 0)
    def _():
        nr = min(NORM_ROWS, tm)
        assert tm % nr == 0

        def norm_rows(c, carry):
            rows = pl.ds(pl.multiple_of(c * nr, nr), nr)
            x = x_ref[rows, :]
            xh = x * lax.rsqrt(jnp.mean(x * x, axis=-1, keepdims=True) + EPS)
            for gi in range(n_gains):
                xn_ref[gi, rows, :] = (xh * g_ref[gi:gi + 1, :]).astype(BF16)
            return carry

        lax.fori_loop(0, tm // nr, norm_rows, 0)

    if n_gains == 1:
        res = _dot(xn_ref[0], w_ref[...])
        o_ref[...] = res
        if kv_rows:
            k_rows_ref, v_rows_ref = refs[1], refs[2]
            wkv = KV_A * DH_A

            @pl.when(j == 0)
            def _():
                for h in range(KV_A):
                    k_rows_ref[pl.ds(h, tm, stride=KV_A), :] = res[:, h * DH_A:(h + 1) * DH_A]
                    v_rows_ref[pl.ds(h, tm, stride=KV_A), :] = res[:, wkv + h * DH_A:wkv + (h + 1) * DH_A]
    else:
        tn = o_ref.shape[1]
        split_tile, off = divmod(split, tn)

        @pl.when(j < split_tile)
        def _():
            o_ref[...] = _dot(xn_ref[0], w_ref[...])

        @pl.when(j == split_tile)
        def _():
            if off:
                o_ref[:, :off] = _dot(xn_ref[0], w_ref[:, :off])
            o_ref[:, off:] = _dot(xn_ref[1], w_ref[:, off:])

        @pl.when(j > split_tile)
        def _():
            o_ref[...] = _dot(xn_ref[1], w_ref[...])


def _inproj(x, gains, w, *, tm, tn, split=0, kv_rows=False):
    t, d = x.shape
    n = w.shape[1]
    n_gains = gains.shape[0]
    kern = functools.partial(_inproj_kernel, n_gains=n_gains, split=split, kv_rows=kv_rows)
    out_specs = [pl.BlockSpec((tm, tn), lambda i, j: (i, j))]
    out_shape = [jax.ShapeDtypeStruct((t, n), F32)]
    if kv_rows:
        assert n_gains == 1 and tn == 2 * KV_A * DH_A
        for _ in range(2):
            out_specs.append(pl.BlockSpec((tm * KV_A, DH_A), lambda i, j: (i, 0)))
            out_shape.append(jax.ShapeDtypeStruct((t * KV_A, DH_A), F32))
    return pl.pallas_call(
        kern,
        grid=(t // tm, n // tn),
        in_specs=[
            pl.BlockSpec((tm, d), lambda i, j: (i, 0)),
            pl.BlockSpec((n_gains, d), lambda i, j: (0, 0)),
            pl.BlockSpec((d, tn), lambda i, j: (0, j)),
        ],
        out_specs=out_specs,
        out_shape=out_shape,
        scratch_shapes=[pltpu.VMEM((n_gains, tm, d), BF16)],
        compiler_params=_cparams(("parallel", "arbitrary")),
        name="inproj",
    )(x, gains, w)


def _outproj_kernel(*refs, gated, final_norm):
    refs = list(refs)
    h_ref = refs.pop(0)
    o_ref = refs.pop(0)
    g_ref = refs.pop(0) if gated else None
    w_ref = refs.pop(0)
    nf_ref = refs.pop(0) if final_norm else None
    out_ref = refs.pop(0)

    if gated:
        og = (o_ref[...] * _silu(g_ref[...])).astype(BF16)
    else:
        og = o_ref[...]
    h = h_ref[...] + _dot(og, w_ref[...])
    if final_norm:
        h = h * lax.rsqrt(jnp.mean(h * h, axis=-1, keepdims=True) + EPS) * nf_ref[...]
    out_ref[...] = h


def _outproj(h, o, gate, w, norm_f, *, tm, gate_col_block=0):
    t, d = h.shape
    gated = gate is not None
    final_norm = norm_f is not None
    args = [h, o]
    in_specs = [
        pl.BlockSpec((tm, d), lambda i: (i, 0)),
        pl.BlockSpec((tm, d), lambda i: (i, 0)),
    ]
    if gated:
        args.append(gate)
        in_specs.append(pl.BlockSpec((tm, d), lambda i: (i, gate_col_block)))
    args.append(w)
    in_specs.append(pl.BlockSpec((d, d), lambda i: (0, 0)))
    if final_norm:
        args.append(norm_f)
        in_specs.append(pl.BlockSpec((1, d), lambda i: (0, 0)))
    kern = functools.partial(_outproj_kernel, gated=gated, final_norm=final_norm)
    return pl.pallas_call(
        kern,
        grid=(t // tm,),
        in_specs=in_specs,
        out_specs=pl.BlockSpec((tm, d), lambda i: (i, 0)),
        out_shape=jax.ShapeDtypeStruct((t, d), F32),
        compiler_params=_cparams(("parallel",)),
        name="outproj",
    )(*args)


def _sb_prompt_kernel(q_ref, k_ref, v_ref, g_ref, o_ref, acc_ref, carry_ref):
    qi = pl.program_id(2)
    q = q_ref[...] * (DH_A ** -0.5 * LOG2E)
    qs = jnp.concatenate([q[:, g * DH_A:(g + 1) * DH_A] for g in range(G_A)], axis=0).astype(BF16)
    suffix = _suffix_matrix(SB_TK)
    rows = G_A * SB_TQ

    def tiles(kb_new, n, masked):
        start = pl.multiple_of((kb_new - (n - 1)) * SB_TK, SB_TK)
        k = k_ref[pl.ds(start, n * SB_TK), :].astype(BF16)
        v = v_ref[pl.ds(start, n * SB_TK), :].astype(BF16)
        z = _nt_dot(qs, k)
        order = list(reversed(range(n)))
        ls, l1m, after = {}, {}, {}
        valid = None
        for t in order:
            ls[t], l1m[t] = _log2_beta_terms(z[:, t * SB_TK:(t + 1) * SB_TK])
            if masked and t == n - 1:
                qpos = lax.broadcasted_iota(jnp.int32, (rows, SB_TK), 0) & (SB_TQ - 1)
                kpos = lax.broadcasted_iota(jnp.int32, (rows, SB_TK), 1)
                valid = kpos < qpos
                l1m[t] = jnp.where(valid, l1m[t], 0.0)
            after[t] = _dot(l1m[t].astype(BF16), suffix)
        carry = carry_ref[...]
        w = [None] * n
        for t in order:
            wt = jnp.exp2(ls[t] + after[t] + carry)
            if masked and t == n - 1:
                wt = jnp.where(valid, wt, 0.0)
            w[t] = wt.astype(BF16)
            carry = carry + (after[t][:, 0:1] + l1m[t][:, 0:1])
        carry_ref[...] = carry
        acc_ref[...] += _dot(jnp.concatenate(w, axis=1) if n > 1 else w[0], v)

    acc_ref[...] = jnp.zeros_like(acc_ref)
    carry_ref[...] = jnp.zeros_like(carry_ref)
    odd = qi & 1

    @pl.when(odd == 0)
    def _():
        tiles(qi, 1, True)

    @pl.when(odd == 1)
    def _():
        tiles(qi, 2, True)

    first = qi - 1 - odd

    def body(it, c):
        tiles(first - 2 * it, 2, False)
        return c

    lax.fori_loop(0, lax.shift_right_logical(qi, 1), body, 0)

    gate = _silu(g_ref[...])
    acc = acc_ref[...]
    o_ref[...] = (jnp.concatenate(
        [acc[g * SB_TQ:(g + 1) * SB_TQ] for g in range(G_A)], axis=1) * gate).astype(BF16)


def _sb_prompt(pa, batch, seq):
    nq = seq // SB_TQ
    wq = G_A * DH_A
    k_col0 = 0
    v_col0 = KV_A
    q_col0 = (2 * KV_A * DH_A) // wq
    g_col0 = q_col0 + KV_A
    return pl.pallas_call(
        _sb_prompt_kernel,
        grid=(batch, KV_A, nq),
        in_specs=[
            pl.BlockSpec((SB_TQ, wq), lambda b, h, i: (b * nq + i, q_col0 + h)),
            pl.BlockSpec((seq, DH_A), lambda b, h, i: (b, k_col0 + h)),
            pl.BlockSpec((seq, DH_A), lambda b, h, i: (b, v_col0 + h)),
            pl.BlockSpec((SB_TQ, wq), lambda b, h, i: (b * nq + i, g_col0 + h)),
        ],
        out_specs=pl.BlockSpec((SB_TQ, wq), lambda b, h, i: (b * nq + i, h)),
        out_shape=jax.ShapeDtypeStruct((batch * seq, KV_A * wq), BF16),
        scratch_shapes=[
            pltpu.VMEM((G_A * SB_TQ, DH_A), F32),
            pltpu.VMEM((G_A * SB_TQ, 1), F32),
        ],
        compiler_params=_cparams(("parallel", "parallel", "arbitrary")),
        name="sb_prompt",
    )(pa, pa, pa, pa)


def _sb_sample_kernel(pt_ref, q_ref, kn_ref, vn_ref, ck_hbm, cv_hbm, o_ref,
                      kbuf, vbuf, sems, acc_ref, carry_ref, *, n_new, n_pages):
    npg = SB_PAGES_PER_STEP
    r = pl.program_id(0)
    c = pl.program_id(1)
    n_chunks = pl.num_programs(1)
    step = r * n_chunks + c
    total = pl.num_programs(0) * n_chunks
    rows_h = G_A * n_new
    rows = KV_A * rows_h
    scale = DH_A ** -0.5 * LOG2E

    def page_copies(s, slot):
        rr = lax.div(s, n_chunks)
        cc = lax.rem(s, n_chunks)
        copies = []
        for i in range(npg):
            pid = pt_ref[rr, n_pages - 1 - (cc * npg + i)]
            src = pl.ds(pl.multiple_of(pid * PAGE_ROWS, PAGE_ROWS), PAGE_ROWS)
            dst = pl.ds(i * PAGE_ROWS, PAGE_ROWS)
            copies.append(pltpu.make_async_copy(ck_hbm.at[src, :], kbuf.at[slot, dst, :], sems.at[0, slot]))
            copies.append(pltpu.make_async_copy(cv_hbm.at[src, :], vbuf.at[slot, dst, :], sems.at[1, slot]))
        return copies

    @pl.when(step == 0)
    def _():
        for s0 in range(SB_RING - 1):
            for cp in page_copies(jnp.int32(s0), s0):
                cp.start()

    ahead = step + (SB_RING - 1)

    @pl.when(ahead < total)
    def _():
        for cp in page_copies(ahead, lax.rem(ahead, SB_RING)):
            cp.start()

    slot = lax.rem(step, SB_RING)
    for cp in page_copies(step, slot):
        cp.wait()
    kpages = kbuf.at[slot]
    vpages = vbuf.at[slot]

    @pl.when(c == 0)
    def _():
        t_of_row = lax.broadcasted_iota(jnp.int32, (rows_h, 1), 0) & (n_new - 1)
        for h in range(KV_A):
            qh = q_ref[0, h] * scale
            kn = kn_ref[0, h]
            vn = vn_ref[0, h]
            ls_cols, l1m_cols = [], []
            for j in range(n_new):
                zj = jnp.sum(qh * kn[j:j + 1, :], axis=-1, keepdims=True)
                ls_j, l1m_j = _log2_beta_terms(zj)
                ls_cols.append(ls_j)
                l1m_cols.append(jnp.where(t_of_row > j, l1m_j, 0.0))
            acc = jnp.zeros((rows_h, DH_A), F32)
            after = jnp.zeros((rows_h, 1), F32)
            for j in reversed(range(n_new)):
                wj = jnp.where(t_of_row > j, jnp.exp2(ls_cols[j] + after), 0.0)
                acc = acc + wj * vn[j:j + 1, :]
                after = after + l1m_cols[j]
            acc_ref[h] = acc
            carry_ref[h * rows_h:(h + 1) * rows_h, :] = jnp.broadcast_to(after, (rows_h, PAGE))

    def head_rows(pages, h):
        return pages[pl.ds(h, npg * PAGE, stride=KV_A), :].astype(BF16)

    z = [_nt_dot((q_ref[0, h] * scale).astype(BF16), head_rows(kpages, h)) for h in range(KV_A)]
    z = jnp.concatenate([z[h][:, i * PAGE:(i + 1) * PAGE] for i in range(npg) for h in range(KV_A)], axis=0)
    ls, l1m = _log2_beta_terms(z)
    suffix = jnp.concatenate([_suffix_matrix(PAGE), jnp.ones((PAGE, PAGE), BF16)], axis=1)
    sums = _dot(l1m.astype(BF16), suffix)
    after = sums[:, :PAGE]
    totals = sums[:, PAGE:]
    carry = carry_ref[...]
    ws = []
    for i in range(npg):
        sl = slice(i * rows, (i + 1) * rows)
        ws.append(jnp.exp2(ls[sl] + after[sl] + carry).astype(BF16))
        carry = carry + totals[sl]
    carry_ref[...] = carry
    for h in range(KV_A):
        w_h = jnp.concatenate([ws[i][h * rows_h:(h + 1) * rows_h] for i in range(npg)], axis=1)
        acc_ref[h] += _dot(w_h, head_rows(vpages, h))

    @pl.when(c == pl.num_programs(1) - 1)
    def _():
        o_ref[0] = acc_ref[...]


def _sb_sample(q, k_new, v_new, cache_k, cache_v, page_table):
    db, _, rows_h, _ = q.shape
    ds = k_new.shape[2]
    n_pages = page_table.shape[1]
    npg = SB_PAGES_PER_STEP
    n_chunks = n_pages // npg

    assert n_pages % npg == 0 and db * n_chunks >= SB_RING - 1
    in_specs = [
        pl.BlockSpec((1, KV_A, rows_h, DH_A), lambda r, c, pt: (r, 0, 0, 0)),
        pl.BlockSpec((1, KV_A, ds, DH_A), lambda r, c, pt: (r, 0, 0, 0)),
        pl.BlockSpec((1, KV_A, ds, DH_A), lambda r, c, pt: (r, 0, 0, 0)),
        pl.BlockSpec(memory_space=pl.ANY),
        pl.BlockSpec(memory_space=pl.ANY),
    ]
    grid_spec = pltpu.PrefetchScalarGridSpec(
        num_scalar_prefetch=1,
        grid=(db, n_chunks),
        in_specs=in_specs,
        out_specs=pl.BlockSpec((1, KV_A, rows_h, DH_A), lambda r, c, pt: (r, 0, 0, 0)),
        scratch_shapes=[
            pltpu.VMEM((SB_RING, npg * PAGE_ROWS, DH_A), F32),
            pltpu.VMEM((SB_RING, npg * PAGE_ROWS, DH_A), F32),
            pltpu.SemaphoreType.DMA((2, SB_RING)),
            pltpu.VMEM((KV_A, rows_h, DH_A), F32),
            pltpu.VMEM((KV_A * rows_h, PAGE), F32),
        ],
    )
    kern = functools.partial(_sb_sample_kernel, n_new=ds, n_pages=n_pages)
    return pl.pallas_call(
        kern,
        grid_spec=grid_spec,
        out_shape=jax.ShapeDtypeStruct((db, KV_A, rows_h, DH_A), F32),
        compiler_params=_cparams(("arbitrary", "arbitrary")),
        name="sb_sample",
    )(page_table, q, k_new, v_new, cache_k, cache_v)


def _sink_attention(qs, k, v, dist, valid, slope, sink):
    s = _nt_dot(qs, k)
    s = jnp.where(valid, s - slope * dist, NEG)
    m = jnp.maximum(jnp.max(s, axis=-1, keepdims=True), sink)
    p = jnp.exp(s - m)
    denom = jnp.sum(p, axis=-1, keepdims=True) + jnp.exp(sink - m)
    return _dot(p.astype(BF16), v) * (1.0 / denom)


def _head_columns(sinks_ref, kvh, rows_per_head):
    g_idx = lax.broadcasted_iota(jnp.int32, (G_B * rows_per_head, 1), 0) // rows_per_head
    slope = jnp.zeros((G_B * rows_per_head, 1), F32)
    sink = jnp.zeros((G_B * rows_per_head, 1), F32)
    for g in range(G_B):
        h = kvh * G_B + g
        slope = jnp.where(g_idx == g, 2.0 ** (-8.0 * (h + 1) / H_B), slope)
        sink = jnp.where(g_idx == g, sinks_ref[h], sink)
    return slope, sink


def _swa_prompt_kernel(sinks_ref, q_ref, g_ref, kvp_ref, kvc_ref, bias_ref, o_ref):
    n = pl.program_id(1)
    blk = WINDOW
    vo = KV_B * DH_B
    for sb in range(SWA_QBLOCKS):
        rows = slice(sb * blk, (sb + 1) * blk)
        q = q_ref[rows, :] * (DH_B ** -0.5 * LOG2E)
        if sb == 0:
            kv = jnp.concatenate([kvp_ref[...], kvc_ref[0:blk, :]], axis=0)
        else:
            kv = kvc_ref[(sb - 1) * blk:(sb + 1) * blk, :]
        outs = []
        for kvh in range(KV_B):
            c0 = kvh * DH_B
            k = kv[:, c0:c0 + DH_B].astype(BF16)
            v = kv[:, vo + c0:vo + c0 + DH_B].astype(BF16)
            qs = jnp.concatenate(
                [q[:, (kvh * G_B + g) * DH_B:(kvh * G_B + g + 1) * DH_B] for g in range(G_B)], axis=0).astype(BF16)
            z = _nt_dot(qs, k)
            ps, inv = [], []
            for g in range(G_B):
                h = kvh * G_B + g
                s = z[g * blk:(g + 1) * blk] + bias_ref[h]
                if sb == 0:
                    no_prev = jnp.where(n > 0, 0.0, NEG)
                    s = jnp.concatenate([s[:, :blk] + no_prev, s[:, blk:]], axis=1)
                sink = sinks_ref[h] * LOG2E
                m = jnp.maximum(jnp.max(s, axis=-1, keepdims=True), sink)
                p = jnp.exp2(s - m)
                inv.append(1.0 / (jnp.sum(p, axis=-1, keepdims=True) + jnp.exp2(sink - m)))
                ps.append(p.astype(BF16))
            o = _dot(jnp.concatenate(ps, axis=0), v)
            outs.extend(o[g * blk:(g + 1) * blk] * inv[g] for g in range(G_B))
        o_ref[rows, :] = (jnp.concatenate(outs, axis=1) * _silu(g_ref[rows, :])).astype(BF16)


def _swa_bias():
    i = jnp.arange(WINDOW)[:, None]
    j = jnp.arange(2 * WINDOW)[None, :]
    dist = i - j + WINDOW
    band = (dist >= 0) & (dist < WINDOW)
    slopes = jnp.exp2(-8.0 * jnp.arange(1, H_B + 1, dtype=F32) / H_B)
    return jnp.where(band[None], -slopes[:, None, None] * dist[None].astype(F32) * LOG2E, NEG)


def _swa_prompt(pb, sinks, batch, seq):
    blk = WINDOW
    qb = SWA_QBLOCKS
    nb = seq // (qb * blk)
    nblk = seq // blk
    wq = H_B * DH_B
    wkv = 2 * KV_B * DH_B
    kv_col = (2 * wq) // wkv
    return pl.pallas_call(
        _swa_prompt_kernel,
        grid=(batch, nb),
        in_specs=[
            pl.BlockSpec(memory_space=pltpu.SMEM),
            pl.BlockSpec((qb * blk, wq), lambda b, n: (b * nb + n, 0)),
            pl.BlockSpec((qb * blk, wq), lambda b, n: (b * nb + n, 1)),
            pl.BlockSpec((blk, wkv), lambda b, n: (b * nblk + jnp.maximum(qb * n - 1, 0), kv_col)),
            pl.BlockSpec((qb * blk, wkv), lambda b, n: (b * nb + n, kv_col)),
            pl.BlockSpec((H_B, blk, 2 * blk), lambda b, n: (0, 0, 0)),
        ],
        out_specs=pl.BlockSpec((qb * blk, wq), lambda b, n: (b * nb + n, 0)),
        out_shape=jax.ShapeDtypeStruct((batch * seq, wq), BF16),
        compiler_params=_cparams(("parallel", "arbitrary")),
        name="swa_prompt",
    )(sinks, pb, pb, pb, pb, _swa_bias())


def _swa_sample_kernel(sinks_ref, q_ref, kc_ref, vc_ref, kn_ref, vn_ref, o_ref, *, n_new):
    n_buf = kc_ref.shape[1]
    rows = G_B * n_new
    keys = 2 * n_buf
    t = lax.broadcasted_iota(jnp.int32, (rows, keys), 0) & (n_new - 1)
    j = lax.broadcasted_iota(jnp.int32, (rows, keys), 1)
    dist = n_buf + t - j
    valid = (dist >= 0) & (dist < WINDOW)
    distf = dist.astype(F32)
    pad = jnp.zeros((n_buf - kn_ref.shape[1], DH_B), F32)
    for r in range(SWA_REQ_PER_STEP):
        kc = kc_ref[r]
        vc = vc_ref[r]
        kn = kn_ref[r]
        vn = vn_ref[r]
        for kvh in range(KV_B):
            c0 = kvh * DH_B
            k = jnp.concatenate([kc[:, c0:c0 + DH_B], kn[:, c0:c0 + DH_B], pad], axis=0).astype(BF16)
            v = jnp.concatenate([vc[:, c0:c0 + DH_B], vn[:, c0:c0 + DH_B], pad], axis=0).astype(BF16)
            qs = (q_ref[r, kvh] * (DH_B ** -0.5)).astype(BF16)
            slope, sink = _head_columns(sinks_ref, kvh, n_new)
            o_ref[r, kvh] = _sink_attention(qs, k, v, distf, valid, slope, sink)


def _swa_sample(q, cache_k, cache_v, k_new, v_new, sinks):
    db, _, rows, _ = q.shape
    n_buf = cache_k.shape[1]
    ds = k_new.shape[1]
    wkv = KV_B * DH_B
    rq = SWA_REQ_PER_STEP
    ds_pad = -(-ds // 8) * 8
    k_new = jnp.pad(k_new, ((0, 0), (0, ds_pad - ds), (0, 0)))
    v_new = jnp.pad(v_new, ((0, 0), (0, ds_pad - ds), (0, 0)))
    kern = functools.partial(_swa_sample_kernel, n_new=ds)
    return pl.pallas_call(
        kern,
        grid=(db // rq,),
        in_specs=[
            pl.BlockSpec(memory_space=pltpu.SMEM),
            pl.BlockSpec((rq, KV_B, rows, DH_B), lambda i: (i, 0, 0, 0)),
            pl.BlockSpec((rq, n_buf, wkv), lambda i: (i, 0, 0)),
            pl.BlockSpec((rq, n_buf, wkv), lambda i: (i, 0, 0)),
            pl.BlockSpec((rq, ds_pad, wkv), lambda i: (i, 0, 0)),
            pl.BlockSpec((rq, ds_pad, wkv), lambda i: (i, 0, 0)),
        ],
        out_specs=pl.BlockSpec((rq, KV_B, rows, DH_B), lambda i: (i, 0, 0, 0)),
        out_shape=jax.ShapeDtypeStruct((db, KV_B, rows, DH_B), F32),
        compiler_params=_cparams(("parallel",)),
        name="swa_sample",
    )(sinks, q, cache_k, cache_v, k_new, v_new)


def kernel(x_prompt, x_sample, cache_a_k, cache_a_v, cache_b_k, cache_b_v, page_table,
           norm_a, w_in_a, w_out_a, norm_kv, w_kv, norm_b, w_in_b, sinks_b, w_out_b, norm_f):
    batch, seq, d = x_prompt.shape
    db, ds, _ = x_sample.shape
    w_buf = cache_b_k.shape[1]
    n_pool, page = cache_a_k.shape[1], cache_a_k.shape[2]
    wq_a = KV_A * G_A * DH_A
    wkv_a = KV_A * DH_A
    wq_b = H_B * DH_B
    wkv_b = KV_B * DH_B

    w_a = w_in_a[0]
    w_in_a0 = jnp.concatenate([w_a[:, wq_a:wq_a + 2 * wkv_a], w_a[:, :wq_a], w_a[:, wq_a + 2 * wkv_a:]],
                              axis=1).astype(BF16)
    q_lo, g_lo = 2 * wkv_a, 2 * wkv_a + wq_a
    w_out_a0 = w_out_a[0].astype(BF16)
    w_in_bkv = jnp.concatenate([w_in_b[0], w_kv], axis=1).astype(BF16)
    w_out_b0 = w_out_b[0].astype(BF16)
    gains_b = jnp.stack([norm_b[0], norm_kv], axis=0)
    nf = norm_f.reshape(1, d)
    sinks = sinks_b[0]
    split_b = 2 * wq_b
    tn_b = (2 * wq_b + 2 * wkv_b) // 3

    xp = x_prompt.reshape(batch * seq, d)
    pa, k_rows, v_rows = _inproj(xp, norm_a[0:1], w_in_a0, tm=1024, tn=2 * wkv_a, kv_rows=True)
    a_k_prompt = k_rows.reshape(1, batch, seq, KV_A, DH_A)
    a_v_prompt = v_rows.reshape(1, batch, seq, KV_A, DH_A)
    og = _sb_prompt(pa, batch, seq)
    h1 = _outproj(xp, og, None, w_out_a0, None, tm=512)
    pb = _inproj(h1, gains_b, w_in_bkv, tm=1024, tn=tn_b, split=split_b)[0]
    kvb = pb.reshape(batch, seq, 2 * wq_b + 2 * wkv_b)[:, seq - w_buf:, 2 * wq_b:]
    b_k_prompt = kvb[:, :, :wkv_b].reshape(batch, w_buf, KV_B, DH_B)
    b_v_prompt = kvb[:, :, wkv_b:].reshape(batch, w_buf, KV_B, DH_B)
    og2 = _swa_prompt(pb, sinks, batch, seq)
    y_prompt = _outproj(h1, og2, None, w_out_b0, nf, tm=512).reshape(batch, seq, d)

    ts = db * ds
    xs = x_sample.reshape(ts, d)
    pas, k_rows_s, v_rows_s = _inproj(xs, norm_a[0:1], w_in_a0, tm=ts, tn=2 * wkv_a, kv_rows=True)
    a_k_sample = k_rows_s.reshape(1, db, ds, KV_A, DH_A)
    a_v_sample = v_rows_s.reshape(1, db, ds, KV_A, DH_A)
    q_s = pas[:, q_lo:g_lo].reshape(db, ds, KV_A, G_A, DH_A).transpose(0, 2, 3, 1, 4).reshape(
        db, KV_A, G_A * ds, DH_A)
    kn_s = k_rows_s.reshape(db, ds, KV_A, DH_A).transpose(0, 2, 1, 3)
    vn_s = v_rows_s.reshape(db, ds, KV_A, DH_A).transpose(0, 2, 1, 3)
    o_s = _sb_sample(q_s, kn_s, vn_s,
                     cache_a_k[0].reshape(n_pool * page * KV_A, DH_A),
                     cache_a_v[0].reshape(n_pool * page * KV_A, DH_A),
                     page_table)
    o_s = o_s.reshape(db, KV_A, G_A, ds, DH_A).transpose(0, 3, 1, 2, 4).reshape(ts, wq_a)
    h1s = _outproj(xs, o_s, pas[:, g_lo:], w_out_a0, None, tm=ts)
    pbs = _inproj(h1s, gains_b, w_in_bkv, tm=ts, tn=tn_b, split=split_b)[0]
    k_new_b = pbs[:, 2 * wq_b:2 * wq_b + wkv_b].reshape(db, ds, wkv_b)
    v_new_b = pbs[:, 2 * wq_b + wkv_b:].reshape(db, ds, wkv_b)
    q_bs = pbs[:, :wq_b].reshape(db, ds, KV_B, G_B, DH_B).transpose(0, 2, 3, 1, 4).reshape(db, KV_B, G_B * ds, DH_B)
    o_bs = _swa_sample(q_bs, cache_b_k.reshape(db, w_buf, wkv_b), cache_b_v.reshape(db, w_buf, wkv_b),
                       k_new_b, v_new_b, sinks)
    o_bs = o_bs.reshape(db, KV_B, G_B, ds, DH_B).transpose(0, 3, 1, 2, 4).reshape(ts, wq_b)
    y_sample = _outproj(h1s, o_bs, pbs, w_out_b0, nf, tm=ts, gate_col_block=1).reshape(db, ds, d)

    kb_all = jnp.concatenate([cache_b_k, k_new_b.reshape(db, ds, KV_B, DH_B)], axis=1)
    vb_all = jnp.concatenate([cache_b_v, v_new_b.reshape(db, ds, KV_B, DH_B)], axis=1)
    b_k_sample = kb_all[:, -w_buf:]
    b_v_sample = vb_all[:, -w_buf:]

    return (y_prompt, y_sample, a_k_prompt, a_v_prompt, a_k_sample, a_v_sample,
            b_k_prompt, b_v_prompt, b_k_sample, b_v_sample)
```

```python
import functools

import jax
import jax.numpy as jnp
from jax import lax
from jax.experimental import pallas as pl
from jax.experimental.pallas import tpu as pltpu

F32 = jnp.float32
BF16 = jnp.bfloat16

D_MODEL = 2048
DH_A = 128
KV_A = 4
G_A = 4
DH_B = 64
KV_B = 4
G_B = 8
H_B = KV_B * G_B
WINDOW = 128
EPS = 1e-6
NEG = -1e30

VMEM_LIMIT_BYTES = 56 * 1024 * 1024

SB_TQ = 256
SB_TK = 256
SB_PAGES_PER_STEP = 16
SB_RING = 3
PAGE = 128
PAGE_ROWS = PAGE * KV_A
SWA_REQ_PER_STEP = 8
NORM_ROWS = 128


def _cparams(sem):
    return pltpu.CompilerParams(dimension_semantics=sem, vmem_limit_bytes=VMEM_LIMIT_BYTES)


def _nt_dot(a, b):
    return lax.dot_general(a, b, (((1,), (1,)), ((), ())), preferred_element_type=F32)


def _dot(a, b):
    return jnp.dot(a, b, preferred_element_type=F32)


def _silu(g):
    return g * (1.0 / (1.0 + jnp.exp(-g)))


def _suffix_matrix(n):
    row = lax.broadcasted_iota(jnp.int32, (n, n), 0)
    col = lax.broadcasted_iota(jnp.int32, (n, n), 1)
    return jnp.where(row > col, 1.0, 0.0).astype(BF16)


LOG2E = 1.4426950408889634


def _log2_beta_terms(z2):
    lo = jnp.minimum(z2, 0.0)
    hi = lo - z2
    l = jnp.log2(1.0 + jnp.exp2(lo + hi))
    return lo - l, hi - l


def _inproj_kernel(x_ref, g_ref, w_ref, *refs, n_gains, split, kv_rows):
    o_ref, xn_ref = refs[0], refs[-1]
    j = pl.program_id(1)
    tm = x_ref.shape[0]

    @pl.when(j == 0)
    def _():
        nr = min(NORM_ROWS, tm)
        assert tm % nr == 0

        def norm_rows(c, carry):
            rows = pl.ds(pl.multiple_of(c * nr, nr), nr)
            x = x_ref[rows, :]
            xh = x * lax.rsqrt(jnp.mean(x * x, axis=-1, keepdims=True) + EPS)
            for gi in range(n_gains):
                xn_ref[gi, rows, :] = (xh * g_ref[gi:gi + 1, :]).astype(BF16)
            return carry

        lax.fori_loop(0, tm // nr, norm_rows, 0)

    if n_gains == 1:
        res = _dot(xn_ref[0], w_ref[...])
        o_ref[...] = res
        if kv_rows:
            k_rows_ref, v_rows_ref = refs[1], refs[2]
            wkv = KV_A * DH_A

            @pl.when(j == 0)
            def _():
                for h in range(KV_A):
                    k_rows_ref[pl.ds(h, tm, stride=KV_A), :] = res[:, h * DH_A:(h + 1) * DH_A]
                    v_rows_ref[pl.ds(h, tm, stride=KV_A), :] = res[:, wkv + h * DH_A:wkv + (h + 1) * DH_A]
    else:
        tn = o_ref.shape[1]
        split_tile, off = divmod(split, tn)

        @pl.when(j < split_tile)
        def _():
            o_ref[...] = _dot(xn_ref[0], w_ref[...])

        @pl.when(j == split_tile)
        def _():
            if off:
                o_ref[:, :off] = _dot(xn_ref[0], w_ref[:, :off])
            o_ref[:, off:] = _dot(xn_ref[1], w_ref[:, off:])

        @pl.when(j > split_tile)
        def _():
            o_ref[...] = _dot(xn_ref[1], w_ref[...])


def _inproj(x, gains, w, *, tm, tn, split=0, kv_rows=False):
    t, d = x.shape
    n = w.shape[1]
    n_gains = gains.shape[0]
    kern = functools.partial(_inproj_kernel, n_gains=n_gains, split=split, kv_rows=kv_rows)
    out_specs = [pl.BlockSpec((tm, tn), lambda i, j: (i, j))]
    out_shape = [jax.ShapeDtypeStruct((t, n), F32)]
    if kv_rows:
        assert n_gains == 1 and tn == 2 * KV_A * DH_A
        for _ in range(2):
            out_specs.append(pl.BlockSpec((tm * KV_A, DH_A), lambda i, j: (i, 0)))
            out_shape.append(jax.ShapeDtypeStruct((t * KV_A, DH_A), F32))
    return pl.pallas_call(
        kern,
        grid=(t // tm, n // tn),
        in_specs=[
            pl.BlockSpec((tm, d), lambda i, j: (i, 0)),
            pl.BlockSpec((n_gains, d), lambda i, j: (0, 0)),
            pl.BlockSpec((d, tn), lambda i, j: (0, j)),
        ],
        out_specs=out_specs,
        out_shape=out_shape,
        scratch_shapes=[pltpu.VMEM((n_gains, tm, d), BF16)],
        compiler_params=_cparams(("parallel", "arbitrary")),
        name="inproj",
    )(x, gains, w)


def _outproj_kernel(*refs, gated, final_norm):
    refs = list(refs)
    h_ref = refs.pop(0)
    o_ref = refs.pop(0)
    g_ref = refs.pop(0) if gated else None
    w_ref = refs.pop(0)
    nf_ref = refs.pop(0) if final_norm else None
    out_ref = refs.pop(0)

    if gated:
        og = (o_ref[...] * _silu(g_ref[...])).astype(BF16)
    else:
        og = o_ref[...]
    h = h_ref[...] + _dot(og, w_ref[...])
    if final_norm:
        h = h * lax.rsqrt(jnp.mean(h * h, axis=-1, keepdims=True) + EPS) * nf_ref[...]
    out_ref[...] = h


def _outproj(h, o, gate, w, norm_f, *, tm, gate_col_block=0):
    t, d = h.shape
    gated = gate is not None
    final_norm = norm_f is not None
    args = [h, o]
    in_specs = [
        pl.BlockSpec((tm, d), lambda i: (i, 0)),
        pl.BlockSpec((tm, d), lambda i: (i, 0)),
    ]
    if gated:
        args.append(gate)
        in_specs.append(pl.BlockSpec((tm, d), lambda i: (i, gate_col_block)))
    args.append(w)
    in_specs.append(pl.BlockSpec((d, d), lambda i: (0, 0)))
    if final_norm:
        args.append(norm_f)
        in_specs.append(pl.BlockSpec((1, d), lambda i: (0, 0)))
    kern = functools.partial(_outproj_kernel, gated=gated, final_norm=final_norm)
    return pl.pallas_call(
        kern,
        grid=(t // tm,),
        in_specs=in_specs,
        out_specs=pl.BlockSpec((tm, d), lambda i: (i, 0)),
        out_shape=jax.ShapeDtypeStruct((t, d), F32),
        compiler_params=_cparams(("parallel",)),
        name="outproj",
    )(*args)


def _sb_prompt_kernel(q_ref, k_ref, v_ref, g_ref, o_ref, acc_ref, carry_ref):
    qi = pl.program_id(2)
    q = q_ref[...] * (DH_A ** -0.5 * LOG2E)
    qs = jnp.concatenate([q[:, g * DH_A:(g + 1) * DH_A] for g in range(G_A)], axis=0).astype(BF16)
    suffix = _suffix_matrix(SB_TK)
    rows = G_A * SB_TQ

    def tiles(kb_new, n, masked):
        start = pl.multiple_of((kb_new - (n - 1)) * SB_TK, SB_TK)
        k = k_ref[pl.ds(start, n * SB_TK), :].astype(BF16)
        v = v_ref[pl.ds(start, n * SB_TK), :].astype(BF16)
        z = _nt_dot(qs, k)
        order = list(reversed(range(n)))
        ls, l1m, after = {}, {}, {}
        valid = None
        for t in order:
            ls[t], l1m[t] = _log2_beta_terms(z[:, t * SB_TK:(t + 1) * SB_TK])
            if masked and t == n - 1:
                qpos = lax.broadcasted_iota(jnp.int32, (rows, SB_TK), 0) & (SB_TQ - 1)
                kpos = lax.broadcasted_iota(jnp.int32, (rows, SB_TK), 1)
                valid = kpos < qpos
                l1m[t] = jnp.where(valid, l1m[t], 0.0)
            after[t] = _dot(l1m[t].astype(BF16), suffix)
        carry = carry_ref[...]
        w = [None] * n
        for t in order:
            wt = jnp.exp2(ls[t] + after[t] + carry)
            if masked and t == n - 1:
                wt = jnp.where(valid, wt, 0.0)
            w[t] = wt.astype(BF16)
            carry = carry + (after[t][:, 0:1] + l1m[t][:, 0:1])
        carry_ref[...] = carry
        acc_ref[...] += _dot(jnp.concatenate(w, axis=1) if n > 1 else w[0], v)

    acc_ref[...] = jnp.zeros_like(acc_ref)
    carry_ref[...] = jnp.zeros_like(carry_ref)
    odd = qi & 1

    @pl.when(odd == 0)
    def _():
        tiles(qi, 1, True)

    @pl.when(odd == 1)
    def _():
        tiles(qi, 2, True)

    first = qi - 1 - odd

    def body(it, c):
        tiles(first - 2 * it, 2, False)
        return c

    lax.fori_loop(0, lax.shift_right_logical(qi, 1), body, 0)

    gate = _silu(g_ref[...])
    acc = acc_ref[...]
    o_ref[...] = (jnp.concatenate(
        [acc[g * SB_TQ:(g + 1) * SB_TQ] for g in range(G_A)], axis=1) * gate).astype(BF16)


def _sb_prompt(pa, batch, seq):
    nq = seq // SB_TQ
    wq = G_A * DH_A
    k_col0 = 0
    v_col0 = KV_A
    q_col0 = (2 * KV_A * DH_A) // wq
    g_col0 = q_col0 + KV_A
    return pl.pallas_call(
        _sb_prompt_kernel,
        grid=(batch, KV_A, nq),
        in_specs=[
            pl.BlockSpec((SB_TQ, wq), lambda b, h, i: (b * nq + i, q_col0 + h)),
            pl.BlockSpec((seq, DH_A), lambda b, h, i: (b, k_col0 + h)),
            pl.BlockSpec((seq, DH_A), lambda b, h, i: (b, v_col0 + h)),
            pl.BlockSpec((SB_TQ, wq), lambda b, h, i: (b * nq + i, g_col0 + h)),
        ],
        out_specs=pl.BlockSpec((SB_TQ, wq), lambda b, h, i: (b * nq + i, h)),
        out_shape=jax.ShapeDtypeStruct((batch * seq, KV_A * wq), BF16),
        scratch_shapes=[
            pltpu.VMEM((G_A * SB_TQ, DH_A), F32),
            pltpu.VMEM((G_A * SB_TQ, 1), F32),
        ],
        compiler_params=_cparams(("parallel", "parallel", "arbitrary")),
        name="sb_prompt",
    )(pa, pa, pa, pa)


def _sb_sample_kernel(pt_ref, q_ref, kn_ref, vn_ref, ck_hbm, cv_hbm, o_ref,
                      kbuf, vbuf, sems, acc_ref, carry_ref, *, n_new, n_pages):
    npg = SB_PAGES_PER_STEP
    r = pl.program_id(0)
    c = pl.program_id(1)
    n_chunks = pl.num_programs(1)
    step = r * n_chunks + c
    total = pl.num_programs(0) * n_chunks
    rows_h = G_A * n_new
    rows = KV_A * rows_h
    scale = DH_A ** -0.5 * LOG2E

    def page_copies(s, slot):
        rr = lax.div(s, n_chunks)
        cc = lax.rem(s, n_chunks)
        copies = []
        for i in range(npg):
            pid = pt_ref[rr, n_pages - 1 - (cc * npg + i)]
            src = pl.ds(pl.multiple_of(pid * PAGE_ROWS, PAGE_ROWS), PAGE_ROWS)
            dst = pl.ds(i * PAGE_ROWS, PAGE_ROWS)
            copies.append(pltpu.make_async_copy(ck_hbm.at[src, :], kbuf.at[slot, dst, :], sems.at[0, slot]))
            copies.append(pltpu.make_async_copy(cv_hbm.at[src, :], vbuf.at[slot, dst, :], sems.at[1, slot]))
        return copies

    @pl.when(step == 0)
    def _():
        for s0 in range(SB_RING - 1):
            for cp in page_copies(jnp.int32(s0), s0):
                cp.start()

    ahead = step + (SB_RING - 1)

    @pl.when(ahead < total)
    def _():
        for cp in page_copies(ahead, lax.rem(ahead, SB_RING)):
            cp.start()

    slot = lax.rem(step, SB_RING)
    for cp in page_copies(step, slot):
        cp.wait()
    kpages = kbuf.at[slot]
    vpages = vbuf.at[slot]

    @pl.when(c == 0)
    def _():
        t_of_row = lax.broadcasted_iota(jnp.int32, (rows_h, 1), 0) & (n_new - 1)
        for h in range(KV_A):
            qh = q_ref[0, h] * scale
            kn = kn_ref[0, h]
            vn = vn_ref[0, h]
            ls_cols, l1m_cols = [], []
            for j in range(n_new):
                zj = jnp.sum(qh * kn[j:j + 1, :], axis=-1, keepdims=True)
                ls_j, l1m_j = _log2_beta_terms(zj)
                ls_cols.append(ls_j)
                l1m_cols.append(jnp.where(t_of_row > j, l1m_j, 0.0))
            acc = jnp.zeros((rows_h, DH_A), F32)
            after = jnp.zeros((rows_h, 1), F32)
            for j in reversed(range(n_new)):
                wj = jnp.where(t_of_row > j, jnp.exp2(ls_cols[j] + after), 0.0)
                acc = acc + wj * vn[j:j + 1, :]
                after = after + l1m_cols[j]
            acc_ref[h] = acc
            carry_ref[h * rows_h:(h + 1) * rows_h, :] = jnp.broadcast_to(after, (rows_h, PAGE))

    def head_rows(pages, h):
        return pages[pl.ds(h, npg * PAGE, stride=KV_A), :].astype(BF16)

    z = [_nt_dot((q_ref[0, h] * scale).astype(BF16), head_rows(kpages, h)) for h in range(KV_A)]
    z = jnp.concatenate([z[h][:, i * PAGE:(i + 1) * PAGE] for i in range(npg) for h in range(KV_A)], axis=0)
    ls, l1m = _log2_beta_terms(z)
    suffix = jnp.concatenate([_suffix_matrix(PAGE), jnp.ones((PAGE, PAGE), BF16)], axis=1)
    sums = _dot(l1m.astype(BF16), suffix)
    after = sums[:, :PAGE]
    totals = sums[:, PAGE:]
    carry = carry_ref[...]
    ws = []
    for i in range(npg):
        sl = slice(i * rows, (i + 1) * rows)
        ws.append(jnp.exp2(ls[sl] + after[sl] + carry).astype(BF16))
        carry = carry + totals[sl]
    carry_ref[...] = carry
    for h in range(KV_A):
        w_h = jnp.concatenate([ws[i][h * rows_h:(h + 1) * rows_h] for i in range(npg)], axis=1)
        acc_ref[h] += _dot(w_h, head_rows(vpages, h))

    @pl.when(c == pl.num_programs(1) - 1)
    def _():
        o_ref[0] = acc_ref[...]


def _sb_sample(q, k_new, v_new, cache_k, cache_v, page_table):
    db, _, rows_h, _ = q.shape
    ds = k_new.shape[2]
    n_pages = page_table.shape[1]
    npg = SB_PAGES_PER_STEP
    n_chunks = n_pages // npg

    assert n_pages % npg == 0 and db * n_chunks >= SB_RING - 1
    in_specs = [
        pl.BlockSpec((1, KV_A, rows_h, DH_A), lambda r, c, pt: (r, 0, 0, 0)),
        pl.BlockSpec((1, KV_A, ds, DH_A), lambda r, c, pt: (r, 0, 0, 0)),
        pl.BlockSpec((1, KV_A, ds, DH_A), lambda r, c, pt: (r, 0, 0, 0)),
        pl.BlockSpec(memory_space=pl.ANY),
        pl.BlockSpec(memory_space=pl.ANY),
    ]
    grid_spec = pltpu.PrefetchScalarGridSpec(
        num_scalar_prefetch=1,
        grid=(db, n_chunks),
        in_specs=in_specs,
        out_specs=pl.BlockSpec((1, KV_A, rows_h, DH_A), lambda r, c, pt: (r, 0, 0, 0)),
        scratch_shapes=[
            pltpu.VMEM((SB_RING, npg * PAGE_ROWS, DH_A), F32),
            pltpu.VMEM((SB_RING, npg * PAGE_ROWS, DH_A), F32),
            pltpu.SemaphoreType.DMA((2, SB_RING)),
            pltpu.VMEM((KV_A, rows_h, DH_A), F32),
            pltpu.VMEM((KV_A * rows_h, PAGE), F32),
        ],
    )
    kern = functools.partial(_sb_sample_kernel, n_new=ds, n_pages=n_pages)
    return pl.pallas_call(
        kern,
        grid_spec=grid_spec,
        out_shape=jax.ShapeDtypeStruct((db, KV_A, rows_h, DH_A), F32),
        compiler_params=_cparams(("arbitrary", "arbitrary")),
        name="sb_sample",
    )(page_table, q, k_new, v_new, cache_k, cache_v)


def _head_columns(sinks_ref, kvh, rows_per_head):
    g_idx = lax.broadcasted_iota(jnp.int32, (G_B * rows_per_head, 1), 0) // rows_per_head
    slope = jnp.zeros((G_B * rows_per_head, 1), F32)
    sink = jnp.zeros((G_B * rows_per_head, 1), F32)
    for g in range(G_B):
        h = kvh * G_B + g
        slope = jnp.where(g_idx == g, 2.0 ** (-8.0 * (h + 1) / H_B), slope)
        sink = jnp.where(g_idx == g, sinks_ref[h], sink)
    return slope, sink


def _swa_prompt_kernel(sinks_ref, q_ref, g_ref, kvp_ref, kvc_ref, bias_ref, o_ref):
    n = pl.program_id(1)
    blk = WINDOW
    pairs = G_B // 2
    q = (q_ref[...] * (DH_B ** -0.5 * LOG2E)).astype(BF16)
    kv = jnp.concatenate([kvp_ref[...], kvc_ref[...]], axis=0)
    no_prev = jnp.where(n > 0, 0.0, NEG)
    low_kv = lax.broadcasted_iota(jnp.int32, (2 * blk, 2 * DH_B), 1) < DH_B
    low_o = lax.broadcasted_iota(jnp.int32, (blk, 2 * DH_B), 1) < DH_B
    vo = KV_B * DH_B
    outs = []
    for kvh in range(KV_B):
        grp = (kvh // 2) * 2 * DH_B
        own_low = kvh % 2 == 0
        mask = low_kv if own_low else jnp.logical_not(low_kv)
        k_own = jnp.where(mask, kv[:, grp:grp + 2 * DH_B], 0.0)
        v_own = jnp.where(mask, kv[:, vo + grp:vo + grp + 2 * DH_B], 0.0)
        k_oth = pltpu.roll(k_own, DH_B, 1)
        v_oth = pltpu.roll(v_own, DH_B, 1)
        k_lo, k_hi = (k_own, k_oth) if own_low else (k_oth, k_own)
        v_lo, v_hi = (v_own, v_oth) if own_low else (v_oth, v_own)
        qp = jnp.concatenate(
            [q[:, (kvh * pairs + pr) * 2 * DH_B:(kvh * pairs + pr + 1) * 2 * DH_B] for pr in range(pairs)], axis=0)
        z = (_nt_dot(qp, k_lo.astype(BF16)), _nt_dot(qp, k_hi.astype(BF16)))
        ps, inv = ([], []), ([], [])
        for pr in range(pairs):
            for half in range(2):
                h = kvh * G_B + 2 * pr + half
                s = z[half][pr * blk:(pr + 1) * blk] + bias_ref[h]
                s = jnp.concatenate([s[:, :blk] + no_prev, s[:, blk:]], axis=1)
                sink = sinks_ref[h] * LOG2E
                m = jnp.maximum(jnp.max(s, axis=-1, keepdims=True), sink)
                p = jnp.exp2(s - m)
                inv[half].append(1.0 / (jnp.sum(p, axis=-1, keepdims=True) + jnp.exp2(sink - m)))
                ps[half].append(p.astype(BF16))
        o = (_dot(jnp.concatenate(ps[0], axis=0), v_lo.astype(BF16))
             + _dot(jnp.concatenate(ps[1], axis=0), v_hi.astype(BF16)))
        for pr in range(pairs):
            outs.append(o[pr * blk:(pr + 1) * blk] * jnp.where(low_o, inv[0][pr], inv[1][pr]))
    o_ref[...] = (jnp.concatenate(outs, axis=1) * _silu(g_ref[...])).astype(BF16)


def _swa_bias():
    i = jnp.arange(WINDOW)[:, None]
    j = jnp.arange(2 * WINDOW)[None, :]
    dist = i - j + WINDOW
    band = (dist >= 0) & (dist < WINDOW)
    slopes = jnp.exp2(-8.0 * jnp.arange(1, H_B + 1, dtype=F32) / H_B)
    return jnp.where(band[None], -slopes[:, None, None] * dist[None].astype(F32) * LOG2E, NEG)


def _swa_prompt(pb, sinks, batch, seq):
    blk = WINDOW
    nb = seq // blk
    wq = H_B * DH_B
    wkv = 2 * KV_B * DH_B
    kv_col = (2 * wq) // wkv
    return pl.pallas_call(
        _swa_prompt_kernel,
        grid=(batch, nb),
        in_specs=[
            pl.BlockSpec(memory_space=pltpu.SMEM),
            pl.BlockSpec((blk, wq), lambda b, n: (b * nb + n, 0)),
            pl.BlockSpec((blk, wq), lambda b, n: (b * nb + n, 1)),
            pl.BlockSpec((blk, wkv), lambda b, n: (b * nb + jnp.maximum(n - 1, 0), kv_col)),
            pl.BlockSpec((blk, wkv), lambda b, n: (b * nb + n, kv_col)),
            pl.BlockSpec((H_B, blk, 2 * blk), lambda b, n: (0, 0, 0)),
        ],
        out_specs=pl.BlockSpec((blk, wq), lambda b, n: (b * nb + n, 0)),
        out_shape=jax.ShapeDtypeStruct((batch * seq, wq), BF16),
        compiler_params=_cparams(("parallel", "arbitrary")),
        name="swa_prompt",
    )(sinks, pb, pb, pb, pb, _swa_bias())


def _swa_sample_kernel(sinks_ref, q_ref, kct_ref, vct_ref, kn_ref, vn_ref, o_ref, *, n_new):
    n_buf = kct_ref.shape[1]
    rows = G_B * n_new
    keys = 2 * n_buf
    t = lax.broadcasted_iota(jnp.int32, (rows, keys), 0) & (n_new - 1)
    j = lax.broadcasted_iota(jnp.int32, (rows, keys), 1)
    dist = n_buf + t - j
    valid = (dist >= 0) & (dist < WINDOW)
    distf = dist.astype(F32)
    pad = jnp.zeros((n_buf - kn_ref.shape[1], DH_B), F32)
    for r in range(SWA_REQ_PER_STEP):
        kn = kn_ref[r]
        vn = vn_ref[r]
        for kvh in range(KV_B):
            c0 = kvh * DH_B
            base = (r * KV_B + kvh) * DH_B
            kct = kct_ref[base:base + DH_B, :].astype(BF16)
            vct = vct_ref[base:base + DH_B, :].astype(BF16)
            knp = jnp.concatenate([kn[:, c0:c0 + DH_B], pad], axis=0).astype(BF16)
            vnp = jnp.concatenate([vn[:, c0:c0 + DH_B], pad], axis=0).astype(BF16)
            qs = (q_ref[r, kvh] * (DH_B ** -0.5)).astype(BF16)
            slope, sink = _head_columns(sinks_ref, kvh, n_new)
            s = jnp.concatenate([_dot(qs, kct), _nt_dot(qs, knp)], axis=1)
            s = jnp.where(valid, s - slope * distf, NEG)
            m = jnp.maximum(jnp.max(s, axis=-1, keepdims=True), sink)
            p = jnp.exp(s - m)
            denom = jnp.sum(p, axis=-1, keepdims=True) + jnp.exp(sink - m)
            pb = p.astype(BF16)
            o = _nt_dot(pb[:, :n_buf], vct) + _dot(pb[:, n_buf:], vnp)
            o_ref[r, kvh] = o * (1.0 / denom)


def _swa_sample(q, cache_kt, cache_vt, k_new, v_new, sinks):
    db, _, rows, _ = q.shape
    n_buf = cache_kt.shape[1]
    ds = k_new.shape[1]
    wkv = KV_B * DH_B
    rq = SWA_REQ_PER_STEP
    ds_pad = -(-ds // 8) * 8
    k_new = jnp.pad(k_new, ((0, 0), (0, ds_pad - ds), (0, 0)))
    v_new = jnp.pad(v_new, ((0, 0), (0, ds_pad - ds), (0, 0)))
    kern = functools.partial(_swa_sample_kernel, n_new=ds)
    return pl.pallas_call(
        kern,
        grid=(db // rq,),
        in_specs=[
            pl.BlockSpec(memory_space=pltpu.SMEM),
            pl.BlockSpec((rq, KV_B, rows, DH_B), lambda i: (i, 0, 0, 0)),
            pl.BlockSpec((rq * wkv, n_buf), lambda i: (i, 0)),
            pl.BlockSpec((rq * wkv, n_buf), lambda i: (i, 0)),
            pl.BlockSpec((rq, ds_pad, wkv), lambda i: (i, 0, 0)),
            pl.BlockSpec((rq, ds_pad, wkv), lambda i: (i, 0, 0)),
        ],
        out_specs=pl.BlockSpec((rq, KV_B, rows, DH_B), lambda i: (i, 0, 0, 0)),
        out_shape=jax.ShapeDtypeStruct((db, KV_B, rows, DH_B), F32),
        compiler_params=_cparams(("parallel",)),
        name="swa_sample",
    )(sinks, q, cache_kt, cache_vt, k_new, v_new)


def kernel(x_prompt, x_sample, cache_a_k, cache_a_v, cache_b_k, cache_b_v, page_table,
           norm_a, w_in_a, w_out_a, norm_kv, w_kv, norm_b, w_in_b, sinks_b, w_out_b, norm_f):
    batch, seq, d = x_prompt.shape
    db, ds, _ = x_sample.shape
    w_buf = cache_b_k.shape[1]
    n_pool, page = cache_a_k.shape[1], cache_a_k.shape[2]
    wq_a = KV_A * G_A * DH_A
    wkv_a = KV_A * DH_A
    wq_b = H_B * DH_B
    wkv_b = KV_B * DH_B

    w_a = w_in_a[0]
    w_in_a0 = jnp.concatenate([w_a[:, wq_a:wq_a + 2 * wkv_a], w_a[:, :wq_a], w_a[:, wq_a + 2 * wkv_a:]],
                              axis=1).astype(BF16)
    q_lo, g_lo = 2 * wkv_a, 2 * wkv_a + wq_a
    w_out_a0 = w_out_a[0].astype(BF16)
    w_in_bkv = jnp.concatenate([w_in_b[0], w_kv], axis=1).astype(BF16)
    w_out_b0 = w_out_b[0].astype(BF16)
    gains_b = jnp.stack([norm_b[0], norm_kv], axis=0)
    nf = norm_f.reshape(1, d)
    sinks = sinks_b[0]
    split_b = 2 * wq_b
    tn_b = (2 * wq_b + 2 * wkv_b) // 3

    xp = x_prompt.reshape(batch * seq, d)
    pa, k_rows, v_rows = _inproj(xp, norm_a[0:1], w_in_a0, tm=1024, tn=2 * wkv_a, kv_rows=True)
    a_k_prompt = k_rows.reshape(1, batch, seq, KV_A, DH_A)
    a_v_prompt = v_rows.reshape(1, batch, seq, KV_A, DH_A)
    og = _sb_prompt(pa, batch, seq)
    h1 = _outproj(xp, og, None, w_out_a0, None, tm=512)
    pb = _inproj(h1, gains_b, w_in_bkv, tm=1024, tn=tn_b, split=split_b)[0]
    kvb = pb.reshape(batch, seq, 2 * wq_b + 2 * wkv_b)[:, seq - w_buf:, 2 * wq_b:]
    b_k_prompt = kvb[:, :, :wkv_b].reshape(batch, w_buf, KV_B, DH_B)
    b_v_prompt = kvb[:, :, wkv_b:].reshape(batch, w_buf, KV_B, DH_B)
    og2 = _swa_prompt(pb, sinks, batch, seq)
    y_prompt = _outproj(h1, og2, None, w_out_b0, nf, tm=512).reshape(batch, seq, d)

    ts = db * ds
    xs = x_sample.reshape(ts, d)
    pas, k_rows_s, v_rows_s = _inproj(xs, norm_a[0:1], w_in_a0, tm=ts, tn=2 * wkv_a, kv_rows=True)
    a_k_sample = k_rows_s.reshape(1, db, ds, KV_A, DH_A)
    a_v_sample = v_rows_s.reshape(1, db, ds, KV_A, DH_A)
    q_s = pas[:, q_lo:g_lo].reshape(db, ds, KV_A, G_A, DH_A).transpose(0, 2, 3, 1, 4).reshape(
        db, KV_A, G_A * ds, DH_A)
    kn_s = k_rows_s.reshape(db, ds, KV_A, DH_A).transpose(0, 2, 1, 3)
    vn_s = v_rows_s.reshape(db, ds, KV_A, DH_A).transpose(0, 2, 1, 3)
    o_s = _sb_sample(q_s, kn_s, vn_s,
                     cache_a_k[0].reshape(n_pool * page * KV_A, DH_A),
                     cache_a_v[0].reshape(n_pool * page * KV_A, DH_A),
                     page_table)
    o_s = o_s.reshape(db, KV_A, G_A, ds, DH_A).transpose(0, 3, 1, 2, 4).reshape(ts, wq_a)
    h1s = _outproj(xs, o_s, pas[:, g_lo:], w_out_a0, None, tm=ts)
    pbs = _inproj(h1s, gains_b, w_in_bkv, tm=ts, tn=tn_b, split=split_b)[0]
    k_new_b = pbs[:, 2 * wq_b:2 * wq_b + wkv_b].reshape(db, ds, wkv_b)
    v_new_b = pbs[:, 2 * wq_b + wkv_b:].reshape(db, ds, wkv_b)
    q_bs = pbs[:, :wq_b].reshape(db, ds, KV_B, G_B, DH_B).transpose(0, 2, 3, 1, 4).reshape(db, KV_B, G_B * ds, DH_B)
    kt_b = cache_b_k.transpose(0, 2, 3, 1).reshape(db * wkv_b, w_buf)
    vt_b = cache_b_v.transpose(0, 2, 3, 1).reshape(db * wkv_b, w_buf)
    o_bs = _swa_sample(q_bs, kt_b, vt_b, k_new_b, v_new_b, sinks)
    o_bs = o_bs.reshape(db, KV_B, G_B, ds, DH_B).transpose(0, 3, 1, 2, 4).reshape(ts, wq_b)
    y_sample = _outproj(h1s, o_bs, pbs, w_out_b0, nf, tm=ts, gate_col_block=1).reshape(db, ds, d)

    kb_all = jnp.concatenate([cache_b_k, k_new_b.reshape(db, ds, KV_B, DH_B)], axis=1)
    vb_all = jnp.concatenate([cache_b_v, v_new_b.reshape(db, ds, KV_B, DH_B)], axis=1)
    b_k_sample = kb_all[:, -w_buf:]
    b_v_sample = vb_all[:, -w_buf:]

    return (y_prompt, y_sample, a_k_prompt, a_v_prompt, a_k_sample, a_v_sample,
            b_k_prompt, b_v_prompt, b_k_sample, b_v_sample)
```

```python
import functools

import jax
import jax.numpy as jnp
from jax import lax
from jax.experimental import pallas as pl
from jax.experimental.pallas import tpu as pltpu

F32 = jnp.float32
BF16 = jnp.bfloat16

D_MODEL = 2048
DH_A = 128
KV_A = 4
G_A = 4
DH_B = 64
KV_B = 4
G_B = 8
H_B = KV_B * G_B
WINDOW = 128
EPS = 1e-6
NEG = -1e30

VMEM_LIMIT_BYTES = 56 * 1024 * 1024

SB_TQ = 256
SB_TK = 256
SB_PAGES_PER_STEP = 16
SB_RING = 3
PAGE = 128
PAGE_ROWS = PAGE * KV_A
SWA_REQ_PER_STEP = 8
NORM_ROWS = 128


def _cparams(sem):
    return pltpu.CompilerParams(dimension_semantics=sem, vmem_limit_bytes=VMEM_LIMIT_BYTES)


def _nt_dot(a, b):
    return lax.dot_general(a, b, (((1,), (1,)), ((), ())), preferred_element_type=F32)


def _dot(a, b):
    return jnp.dot(a, b, preferred_element_type=F32)


def _silu(g):
    return g * (1.0 / (1.0 + jnp.exp(-g)))


def _suffix_matrix(n):
    row = lax.broadcasted_iota(jnp.int32, (n, n), 0)
    col = lax.broadcasted_iota(jnp.int32, (n, n), 1)
    return jnp.where(row > col, 1.0, 0.0).astype(BF16)


LOG2E = 1.4426950408889634


def _log2_beta_terms(z2):
    lo = jnp.minimum(z2, 0.0)
    hi = lo - z2
    l = jnp.log2(1.0 + jnp.exp2(lo + hi))
    return lo - l, hi - l


def _inproj_kernel(x_ref, g_ref, w_ref, *refs, n_gains, split, kv_rows):
    o_ref, xn_ref = refs[0], refs[-1]
    j = pl.program_id(1)
    tm = x_ref.shape[0]

    @pl.when(j == 0)
    def _():
        nr = min(NORM_ROWS, tm)
        assert tm % nr == 0

        def norm_rows(c, carry):
            rows = pl.ds(pl.multiple_of(c * nr, nr), nr)
            x = x_ref[rows, :]
            xh = x * lax.rsqrt(jnp.mean(x * x, axis=-1, keepdims=True) + EPS)
            for gi in range(n_gains):
                xn_ref[gi, rows, :] = (xh * g_ref[gi:gi + 1, :]).astype(BF16)
            return carry

        lax.fori_loop(0, tm // nr, norm_rows, 0)

    if n_gains == 1:
        res = _dot(xn_ref[0], w_ref[...])
        o_ref[...] = res
        if kv_rows:
            k_rows_ref, v_rows_ref = refs[1], refs[2]
            wkv = KV_A * DH_A

            @pl.when(j == 0)
            def _():
                for h in range(KV_A):
                    k_rows_ref[pl.ds(h, tm, stride=KV_A), :] = res[:, h * DH_A:(h + 1) * DH_A]
                    v_rows_ref[pl.ds(h, tm, stride=KV_A), :] = res[:, wkv + h * DH_A:wkv + (h + 1) * DH_A]
    else:
        tn = o_ref.shape[1]
        split_tile, off = divmod(split, tn)

        @pl.when(j < split_tile)
        def _():
            o_ref[...] = _dot(xn_ref[0], w_ref[...])

        @pl.when(j == split_tile)
        def _():
            if off:
                o_ref[:, :off] = _dot(xn_ref[0], w_ref[:, :off])
            o_ref[:, off:] = _dot(xn_ref[1], w_ref[:, off:])

        @pl.when(j > split_tile)
        def _():
            o_ref[...] = _dot(xn_ref[1], w_ref[...])


def _inproj(x, gains, w, *, tm, tn, split=0, kv_rows=False):
    t, d = x.shape
    n = w.shape[1]
    n_gains = gains.shape[0]
    kern = functools.partial(_inproj_kernel, n_gains=n_gains, split=split, kv_rows=kv_rows)
    out_specs = [pl.BlockSpec((tm, tn), lambda i, j: (i, j))]
    out_shape = [jax.ShapeDtypeStruct((t, n), F32)]
    if kv_rows:
        assert n_gains == 1 and tn == 2 * KV_A * DH_A
        for _ in range(2):
            out_specs.append(pl.BlockSpec((tm * KV_A, DH_A), lambda i, j: (i, 0)))
            out_shape.append(jax.ShapeDtypeStruct((t * KV_A, DH_A), F32))
    return pl.pallas_call(
        kern,
        grid=(t // tm, n // tn),
        in_specs=[
            pl.BlockSpec((tm, d), lambda i, j: (i, 0)),
            pl.BlockSpec((n_gains, d), lambda i, j: (0, 0)),
            pl.BlockSpec((d, tn), lambda i, j: (0, j)),
        ],
        out_specs=out_specs,
        out_shape=out_shape,
        scratch_shapes=[pltpu.VMEM((n_gains, tm, d), BF16)],
        compiler_params=_cparams(("parallel", "arbitrary")),
        name="inproj",
    )(x, gains, w)


def _outproj_kernel(*refs, gated, final_norm):
    refs = list(refs)
    h_ref = refs.pop(0)
    o_ref = refs.pop(0)
    g_ref = refs.pop(0) if gated else None
    w_ref = refs.pop(0)
    nf_ref = refs.pop(0) if final_norm else None
    out_ref = refs.pop(0)

    if gated:
        og = (o_ref[...] * _silu(g_ref[...])).astype(BF16)
    else:
        og = o_ref[...]
    h = h_ref[...] + _dot(og, w_ref[...])
    if final_norm:
        h = h * lax.rsqrt(jnp.mean(h * h, axis=-1, keepdims=True) + EPS) * nf_ref[...]
    out_ref[...] = h


def _outproj(h, o, gate, w, norm_f, *, tm, gate_col_block=0):
    t, d = h.shape
    gated = gate is not None
    final_norm = norm_f is not None
    args = [h, o]
    in_specs = [
        pl.BlockSpec((tm, d), lambda i: (i, 0)),
        pl.BlockSpec((tm, d), lambda i: (i, 0)),
    ]
    if gated:
        args.append(gate)
        in_specs.append(pl.BlockSpec((tm, d), lambda i: (i, gate_col_block)))
    args.append(w)
    in_specs.append(pl.BlockSpec((d, d), lambda i: (0, 0)))
    if final_norm:
        args.append(norm_f)
        in_specs.append(pl.BlockSpec((1, d), lambda i: (0, 0)))
    kern = functools.partial(_outproj_kernel, gated=gated, final_norm=final_norm)
    return pl.pallas_call(
        kern,
        grid=(t // tm,),
        in_specs=in_specs,
        out_specs=pl.BlockSpec((tm, d), lambda i: (i, 0)),
        out_shape=jax.ShapeDtypeStruct((t, d), F32),
        compiler_params=_cparams(("parallel",)),
        name="outproj",
    )(*args)


def _sb_prompt_kernel(q_ref, k_ref, v_ref, g_ref, o_ref, acc_ref, carry_ref):
    suffix = _suffix_matrix(SB_TK)
    rows = G_A * SB_TQ
    nq = q_ref.shape[0] // SB_TQ

    hq = SB_TQ // 2
    assert SB_TK == SB_TQ
    tri = lax.broadcasted_iota(jnp.int32, (hq, hq), 1) < lax.broadcasted_iota(jnp.int32, (hq, hq), 0)
    zero_q = jnp.zeros((hq, hq), F32)

    def diag_blocks(x, g):
        r0 = g * SB_TQ
        return x[r0:r0 + hq, :hq], x[r0 + hq:r0 + SB_TQ, :hq], x[r0 + hq:r0 + SB_TQ, hq:]

    def assemble(blocks):
        return jnp.concatenate([
            jnp.concatenate([jnp.concatenate([a, zero_q], axis=1), jnp.concatenate([b, c], axis=1)], axis=0)
            for a, b, c in blocks], axis=0)

    def diag_terms(zt):
        ls_blocks, l1m_blocks = [], []
        for g in range(G_A):
            (lsa, la), (lsb, lb), (lsc, lc) = [_log2_beta_terms(zb) for zb in diag_blocks(zt, g)]
            ls_blocks.append((lsa, lsb, lsc))
            l1m_blocks.append((jnp.where(tri, la, 0.0), lb, jnp.where(tri, lc, 0.0)))
        return ls_blocks, assemble(l1m_blocks)

    def diag_weights(ls_blocks, after_t, carry):
        out = []
        for g in range(G_A):
            aa, ab, ac = diag_blocks(after_t, g)
            r0 = g * SB_TQ
            c_early, c_late = carry[r0:r0 + hq], carry[r0 + hq:r0 + SB_TQ]
            lsa, lsb, lsc = ls_blocks[g]
            out.append((jnp.where(tri, jnp.exp2(lsa + aa + c_early), 0.0),
                        jnp.exp2(lsb + ab + c_late),
                        jnp.where(tri, jnp.exp2(lsc + ac + c_late), 0.0)))
        return assemble(out)

    def q_tile(qi, unused):
        qrows = pl.ds(pl.multiple_of(qi * SB_TQ, SB_TQ), SB_TQ)
        q = q_ref[qrows, :] * (DH_A ** -0.5 * LOG2E)
        qs = jnp.concatenate([q[:, g * DH_A:(g + 1) * DH_A] for g in range(G_A)], axis=0).astype(BF16)

        def tiles(kb_new, n, masked):
            start = pl.multiple_of((kb_new - (n - 1)) * SB_TK, SB_TK)
            k = k_ref[pl.ds(start, n * SB_TK), :].astype(BF16)
            v = v_ref[pl.ds(start, n * SB_TK), :].astype(BF16)
            z = _nt_dot(qs, k)
            order = list(reversed(range(n)))
            ls, l1m, after = {}, {}, {}
            for t in order:
                zt = z[:, t * SB_TK:(t + 1) * SB_TK]
                if masked and t == n - 1:
                    ls[t], l1m[t] = diag_terms(zt)
                else:
                    ls[t], l1m[t] = _log2_beta_terms(zt)
                after[t] = _dot(l1m[t].astype(BF16), suffix)
            carry = carry_ref[...]
            w = [None] * n
            for t in order:
                if masked and t == n - 1:
                    wt = diag_weights(ls[t], after[t], carry)
                else:
                    wt = jnp.exp2(ls[t] + after[t] + carry)
                w[t] = wt.astype(BF16)
                carry = carry + (after[t][:, 0:1] + l1m[t][:, 0:1])
            carry_ref[...] = carry
            acc_ref[...] += _dot(jnp.concatenate(w, axis=1) if n > 1 else w[0], v)

        acc_ref[...] = jnp.zeros_like(acc_ref)
        carry_ref[...] = jnp.zeros_like(carry_ref)
        odd = qi & 1

        @pl.when(odd == 0)
        def _():
            tiles(qi, 1, True)

        @pl.when(odd == 1)
        def _():
            tiles(qi, 2, True)

        first = qi - 1 - odd

        def body(it, c):
            tiles(first - 2 * it, 2, False)
            return c

        lax.fori_loop(0, lax.shift_right_logical(qi, 1), body, 0)

        gate = _silu(g_ref[qrows, :])
        acc = acc_ref[...]
        o_ref[qrows, :] = (jnp.concatenate(
            [acc[g * SB_TQ:(g + 1) * SB_TQ] for g in range(G_A)], axis=1) * gate).astype(BF16)
        return unused

    lax.fori_loop(0, nq, q_tile, 0)


def _sb_prompt(pa, batch, seq):
    wq = G_A * DH_A
    k_col0 = 0
    v_col0 = KV_A
    q_col0 = (2 * KV_A * DH_A) // wq
    g_col0 = q_col0 + KV_A
    return pl.pallas_call(
        _sb_prompt_kernel,
        grid=(batch, KV_A),
        in_specs=[
            pl.BlockSpec((seq, wq), lambda b, h: (b, q_col0 + h)),
            pl.BlockSpec((seq, DH_A), lambda b, h: (b, k_col0 + h)),
            pl.BlockSpec((seq, DH_A), lambda b, h: (b, v_col0 + h)),
            pl.BlockSpec((seq, wq), lambda b, h: (b, g_col0 + h)),
        ],
        out_specs=pl.BlockSpec((seq, wq), lambda b, h: (b, h)),
        out_shape=jax.ShapeDtypeStruct((batch * seq, KV_A * wq), BF16),
        scratch_shapes=[
            pltpu.VMEM((G_A * SB_TQ, DH_A), F32),
            pltpu.VMEM((G_A * SB_TQ, 1), F32),
        ],
        compiler_params=_cparams(("parallel", "parallel")),
        name="sb_prompt",
    )(pa, pa, pa, pa)


def _sb_sample_kernel(pt_ref, q_ref, kn_ref, vn_ref, ck_hbm, cv_hbm, o_ref,
                      kbuf, vbuf, sems, acc_ref, carry_ref, *, n_new, n_pages):
    npg = SB_PAGES_PER_STEP
    r = pl.program_id(0)
    c = pl.program_id(1)
    n_chunks = pl.num_programs(1)
    step = r * n_chunks + c
    total = pl.num_programs(0) * n_chunks
    rows_h = G_A * n_new
    rows = KV_A * rows_h
    scale = DH_A ** -0.5 * LOG2E

    def page_copies(s, slot):
        rr = lax.div(s, n_chunks)
        cc = lax.rem(s, n_chunks)
        copies = []
        for i in range(npg):
            pid = pt_ref[rr, n_pages - 1 - (cc * npg + i)]
            src = pl.ds(pl.multiple_of(pid * PAGE_ROWS, PAGE_ROWS), PAGE_ROWS)
            dst = pl.ds(i * PAGE_ROWS, PAGE_ROWS)
            copies.append(pltpu.make_async_copy(ck_hbm.at[src, :], kbuf.at[slot, dst, :], sems.at[0, slot]))
            copies.append(pltpu.make_async_copy(cv_hbm.at[src, :], vbuf.at[slot, dst, :], sems.at[1, slot]))
        return copies

    @pl.when(step == 0)
    def _():
        for s0 in range(SB_RING - 1):
            for cp in page_copies(jnp.int32(s0), s0):
                cp.start()

    ahead = step + (SB_RING - 1)

    @pl.when(ahead < total)
    def _():
        for cp in page_copies(ahead, lax.rem(ahead, SB_RING)):
            cp.start()

    slot = lax.rem(step, SB_RING)
    for cp in page_copies(step, slot):
        cp.wait()
    kpages = kbuf.at[slot]
    vpages = vbuf.at[slot]

    @pl.when(c == 0)
    def _():
        t_of_row = lax.broadcasted_iota(jnp.int32, (rows_h, 1), 0) & (n_new - 1)
        for h in range(KV_A):
            qh = q_ref[0, h] * scale
            kn = kn_ref[0, h]
            vn = vn_ref[0, h]
            ls_cols, l1m_cols = [], []
            for j in range(n_new):
                zj = jnp.sum(qh * kn[j:j + 1, :], axis=-1, keepdims=True)
                ls_j, l1m_j = _log2_beta_terms(zj)
                ls_cols.append(ls_j)
                l1m_cols.append(jnp.where(t_of_row > j, l1m_j, 0.0))
            acc = jnp.zeros((rows_h, DH_A), F32)
            after = jnp.zeros((rows_h, 1), F32)
            for j in reversed(range(n_new)):
                wj = jnp.where(t_of_row > j, jnp.exp2(ls_cols[j] + after), 0.0)
                acc = acc + wj * vn[j:j + 1, :]
                after = after + l1m_cols[j]
            acc_ref[h] = acc
            carry_ref[h * rows_h:(h + 1) * rows_h, :] = jnp.broadcast_to(after, (rows_h, PAGE))

    def head_rows(pages, h):
        return pages[pl.ds(h, npg * PAGE, stride=KV_A), :].astype(BF16)

    z = [_nt_dot((q_ref[0, h] * scale).astype(BF16), head_rows(kpages, h)) for h in range(KV_A)]
    z = jnp.concatenate([z[h][:, i * PAGE:(i + 1) * PAGE] for i in range(npg) for h in range(KV_A)], axis=0)
    ls, l1m = _log2_beta_terms(z)
    suffix = jnp.concatenate([_suffix_matrix(PAGE), jnp.ones((PAGE, PAGE), BF16)], axis=1)
    sums = _dot(l1m.astype(BF16), suffix)
    after = sums[:, :PAGE]
    totals = sums[:, PAGE:]
    carry = carry_ref[...]
    ws = []
    for i in range(npg):
        sl = slice(i * rows, (i + 1) * rows)
        ws.append(jnp.exp2(ls[sl] + after[sl] + carry).astype(BF16))
        carry = carry + totals[sl]
    carry_ref[...] = carry
    for h in range(KV_A):
        w_h = jnp.concatenate([ws[i][h * rows_h:(h + 1) * rows_h] for i in range(npg)], axis=1)
        acc_ref[h] += _dot(w_h, head_rows(vpages, h))

    @pl.when(c == pl.num_programs(1) - 1)
    def _():
        o_ref[0] = acc_ref[...]


def _sb_sample(q, k_new, v_new, cache_k, cache_v, page_table):
    db, _, rows_h, _ = q.shape
    ds = k_new.shape[2]
    n_pages = page_table.shape[1]
    npg = SB_PAGES_PER_STEP
    n_chunks = n_pages // npg

    assert n_pages % npg == 0 and db * n_chunks >= SB_RING - 1
    in_specs = [
        pl.BlockSpec((1, KV_A, rows_h, DH_A), lambda r, c, pt: (r, 0, 0, 0)),
        pl.BlockSpec((1, KV_A, ds, DH_A), lambda r, c, pt: (r, 0, 0, 0)),
        pl.BlockSpec((1, KV_A, ds, DH_A), lambda r, c, pt: (r, 0, 0, 0)),
        pl.BlockSpec(memory_space=pl.ANY),
        pl.BlockSpec(memory_space=pl.ANY),
    ]
    grid_spec = pltpu.PrefetchScalarGridSpec(
        num_scalar_prefetch=1,
        grid=(db, n_chunks),
        in_specs=in_specs,
        out_specs=pl.BlockSpec((1, KV_A, rows_h, DH_A), lambda r, c, pt: (r, 0, 0, 0)),
        scratch_shapes=[
            pltpu.VMEM((SB_RING, npg * PAGE_ROWS, DH_A), F32),
            pltpu.VMEM((SB_RING, npg * PAGE_ROWS, DH_A), F32),
            pltpu.SemaphoreType.DMA((2, SB_RING)),
            pltpu.VMEM((KV_A, rows_h, DH_A), F32),
            pltpu.VMEM((KV_A * rows_h, PAGE), F32),
        ],
    )
    kern = functools.partial(_sb_sample_kernel, n_new=ds, n_pages=n_pages)
    return pl.pallas_call(
        kern,
        grid_spec=grid_spec,
        out_shape=jax.ShapeDtypeStruct((db, KV_A, rows_h, DH_A), F32),
        compiler_params=_cparams(("arbitrary", "arbitrary")),
        name="sb_sample",
    )(page_table, q, k_new, v_new, cache_k, cache_v)


def _head_columns(sinks_ref, kvh, rows_per_head):
    g_idx = lax.broadcasted_iota(jnp.int32, (G_B * rows_per_head, 1), 0) // rows_per_head
    slope = jnp.zeros((G_B * rows_per_head, 1), F32)
    sink = jnp.zeros((G_B * rows_per_head, 1), F32)
    for g in range(G_B):
        h = kvh * G_B + g
        slope = jnp.where(g_idx == g, 2.0 ** (-8.0 * (h + 1) / H_B), slope)
        sink = jnp.where(g_idx == g, sinks_ref[h], sink)
    return slope, sink


def _swa_prompt_kernel(sinks_ref, q_ref, g_ref, kvp_ref, kvc_ref, bias_ref, o_ref):
    n = pl.program_id(1)
    blk = WINDOW
    pairs = G_B // 2
    q = (q_ref[...] * (DH_B ** -0.5 * LOG2E)).astype(BF16)
    kv = jnp.concatenate([kvp_ref[...], kvc_ref[...]], axis=0)
    no_prev = jnp.where(n > 0, 0.0, NEG)
    low_kv = lax.broadcasted_iota(jnp.int32, (2 * blk, 2 * DH_B), 1) < DH_B
    low_o = lax.broadcasted_iota(jnp.int32, (blk, 2 * DH_B), 1) < DH_B
    vo = KV_B * DH_B
    outs = []
    for kvh in range(KV_B):
        grp = (kvh // 2) * 2 * DH_B
        own_low = kvh % 2 == 0
        mask = low_kv if own_low else jnp.logical_not(low_kv)
        k_own = jnp.where(mask, kv[:, grp:grp + 2 * DH_B], 0.0)
        v_own = jnp.where(mask, kv[:, vo + grp:vo + grp + 2 * DH_B], 0.0)
        k_oth = pltpu.roll(k_own, DH_B, 1)
        v_oth = pltpu.roll(v_own, DH_B, 1)
        k_lo, k_hi = (k_own, k_oth) if own_low else (k_oth, k_own)
        v_lo, v_hi = (v_own, v_oth) if own_low else (v_oth, v_own)
        qp = jnp.concatenate(
            [q[:, (kvh * pairs + pr) * 2 * DH_B:(kvh * pairs + pr + 1) * 2 * DH_B] for pr in range(pairs)], axis=0)
        z = (_nt_dot(qp, k_lo.astype(BF16)), _nt_dot(qp, k_hi.astype(BF16)))
        ps, inv = ([], []), ([], [])
        for pr in range(pairs):
            for half in range(2):
                h = kvh * G_B + 2 * pr + half
                s = z[half][pr * blk:(pr + 1) * blk] + bias_ref[h]
                s = jnp.concatenate([s[:, :blk] + no_prev, s[:, blk:]], axis=1)
                sink = sinks_ref[h] * LOG2E
                m = jnp.maximum(jnp.max(s, axis=-1, keepdims=True), sink)
                p = jnp.exp2(s - m)
                inv[half].append(1.0 / (jnp.sum(p, axis=-1, keepdims=True) + jnp.exp2(sink - m)))
                ps[half].append(p.astype(BF16))
        o = (_dot(jnp.concatenate(ps[0], axis=0), v_lo.astype(BF16))
             + _dot(jnp.concatenate(ps[1], axis=0), v_hi.astype(BF16)))
        for pr in range(pairs):
            outs.append(o[pr * blk:(pr + 1) * blk] * jnp.where(low_o, inv[0][pr], inv[1][pr]))
    o_ref[...] = (jnp.concatenate(outs, axis=1) * _silu(g_ref[...])).astype(BF16)


def _swa_bias():
    i = jnp.arange(WINDOW)[:, None]
    j = jnp.arange(2 * WINDOW)[None, :]
    dist = i - j + WINDOW
    band = (dist >= 0) & (dist < WINDOW)
    slopes = jnp.exp2(-8.0 * jnp.arange(1, H_B + 1, dtype=F32) / H_B)
    return jnp.where(band[None], -slopes[:, None, None] * dist[None].astype(F32) * LOG2E, NEG)


def _swa_prompt(pb, sinks, batch, seq):
    blk = WINDOW
    nb = seq // blk
    wq = H_B * DH_B
    wkv = 2 * KV_B * DH_B
    kv_col = (2 * wq) // wkv
    return pl.pallas_call(
        _swa_prompt_kernel,
        grid=(batch, nb),
        in_specs=[
            pl.BlockSpec(memory_space=pltpu.SMEM),
            pl.BlockSpec((blk, wq), lambda b, n: (b * nb + n, 0)),
            pl.BlockSpec((blk, wq), lambda b, n: (b * nb + n, 1)),
            pl.BlockSpec((blk, wkv), lambda b, n: (b * nb + jnp.maximum(n - 1, 0), kv_col)),
            pl.BlockSpec((blk, wkv), lambda b, n: (b * nb + n, kv_col)),
            pl.BlockSpec((H_B, blk, 2 * blk), lambda b, n: (0, 0, 0)),
        ],
        out_specs=pl.BlockSpec((blk, wq), lambda b, n: (b * nb + n, 0)),
        out_shape=jax.ShapeDtypeStruct((batch * seq, wq), BF16),
        compiler_params=_cparams(("parallel", "arbitrary")),
        name="swa_prompt",
    )(sinks, pb, pb, pb, pb, _swa_bias())


def _swa_sample_kernel(sinks_ref, q_ref, kct_ref, vct_ref, kn_ref, vn_ref, o_ref, *, n_new):
    n_buf = kct_ref.shape[1]
    rows = G_B * n_new
    keys = 2 * n_buf
    t = lax.broadcasted_iota(jnp.int32, (rows, keys), 0) & (n_new - 1)
    j = lax.broadcasted_iota(jnp.int32, (rows, keys), 1)
    dist = n_buf + t - j
    valid = (dist >= 0) & (dist < WINDOW)
    distf = dist.astype(F32)
    pad = jnp.zeros((n_buf - kn_ref.shape[1], DH_B), F32)
    for r in range(SWA_REQ_PER_STEP):
        kn = kn_ref[r]
        vn = vn_ref[r]
        for kvh in range(KV_B):
            c0 = kvh * DH_B
            base = (r * KV_B + kvh) * DH_B
            kct = kct_ref[base:base + DH_B, :].astype(BF16)
            vct = vct_ref[base:base + DH_B, :].astype(BF16)
            knp = jnp.concatenate([kn[:, c0:c0 + DH_B], pad], axis=0).astype(BF16)
            vnp = jnp.concatenate([vn[:, c0:c0 + DH_B], pad], axis=0).astype(BF16)
            qs = (q_ref[r, kvh] * (DH_B ** -0.5)).astype(BF16)
            slope, sink = _head_columns(sinks_ref, kvh, n_new)
            s = jnp.concatenate([_dot(qs, kct), _nt_dot(qs, knp)], axis=1)
            s = jnp.where(valid, s - slope * distf, NEG)
            m = jnp.maximum(jnp.max(s, axis=-1, keepdims=True), sink)
            p = jnp.exp(s - m)
            denom = jnp.sum(p, axis=-1, keepdims=True) + jnp.exp(sink - m)
            pb = p.astype(BF16)
            o = _nt_dot(pb[:, :n_buf], vct) + _dot(pb[:, n_buf:], vnp)
            o_ref[r, kvh] = o * (1.0 / denom)


def _swa_sample(q, cache_kt, cache_vt, k_new, v_new, sinks):
    db, _, rows, _ = q.shape
    n_buf = cache_kt.shape[1]
    ds = k_new.shape[1]
    wkv = KV_B * DH_B
    rq = SWA_REQ_PER_STEP
    ds_pad = -(-ds // 8) * 8
    k_new = jnp.pad(k_new, ((0, 0), (0, ds_pad - ds), (0, 0)))
    v_new = jnp.pad(v_new, ((0, 0), (0, ds_pad - ds), (0, 0)))
    kern = functools.partial(_swa_sample_kernel, n_new=ds)
    return pl.pallas_call(
        kern,
        grid=(db // rq,),
        in_specs=[
            pl.BlockSpec(memory_space=pltpu.SMEM),
            pl.BlockSpec((rq, KV_B, rows, DH_B), lambda i: (i, 0, 0, 0)),
            pl.BlockSpec((rq * wkv, n_buf), lambda i: (i, 0)),
            pl.BlockSpec((rq * wkv, n_buf), lambda i: (i, 0)),
            pl.BlockSpec((rq, ds_pad, wkv), lambda i: (i, 0, 0)),
            pl.BlockSpec((rq, ds_pad, wkv), lambda i: (i, 0, 0)),
        ],
        out_specs=pl.BlockSpec((rq, KV_B, rows, DH_B), lambda i: (i, 0, 0, 0)),
        out_shape=jax.ShapeDtypeStruct((db, KV_B, rows, DH_B), F32),
        compiler_params=_cparams(("parallel",)),
        name="swa_sample",
    )(sinks, q, cache_kt, cache_vt, k_new, v_new)


def kernel(x_prompt, x_sample, cache_a_k, cache_a_v, cache_b_k, cache_b_v, page_table,
           norm_a, w_in_a, w_out_a, norm_kv, w_kv, norm_b, w_in_b, sinks_b, w_out_b, norm_f):
    batch, seq, d = x_prompt.shape
    db, ds, _ = x_sample.shape
    w_buf = cache_b_k.shape[1]
    n_pool, page = cache_a_k.shape[1], cache_a_k.shape[2]
    wq_a = KV_A * G_A * DH_A
    wkv_a = KV_A * DH_A
    wq_b = H_B * DH_B
    wkv_b = KV_B * DH_B

    w_a = w_in_a[0]
    w_in_a0 = jnp.concatenate([w_a[:, wq_a:wq_a + 2 * wkv_a], w_a[:, :wq_a], w_a[:, wq_a + 2 * wkv_a:]],
                              axis=1).astype(BF16)
    q_lo, g_lo = 2 * wkv_a, 2 * wkv_a + wq_a
    w_out_a0 = w_out_a[0].astype(BF16)
    w_in_bkv = jnp.concatenate([w_in_b[0], w_kv], axis=1).astype(BF16)
    w_out_b0 = w_out_b[0].astype(BF16)
    gains_b = jnp.stack([norm_b[0], norm_kv], axis=0)
    nf = norm_f.reshape(1, d)
    sinks = sinks_b[0]
    split_b = 2 * wq_b
    tn_b = (2 * wq_b + 2 * wkv_b) // 3

    xp = x_prompt.reshape(batch * seq, d)
    pa, k_rows, v_rows = _inproj(xp, norm_a[0:1], w_in_a0, tm=1024, tn=2 * wkv_a, kv_rows=True)
    a_k_prompt = k_rows.reshape(1, batch, seq, KV_A, DH_A)
    a_v_prompt = v_rows.reshape(1, batch, seq, KV_A, DH_A)
    og = _sb_prompt(pa, batch, seq)
    h1 = _outproj(xp, og, None, w_out_a0, None, tm=512)
    pb = _inproj(h1, gains_b, w_in_bkv, tm=1024, tn=tn_b, split=split_b)[0]
    kvb = pb.reshape(batch, seq, 2 * wq_b + 2 * wkv_b)[:, seq - w_buf:, 2 * wq_b:]
    b_k_prompt = kvb[:, :, :wkv_b].reshape(batch, w_buf, KV_B, DH_B)
    b_v_prompt = kvb[:, :, wkv_b:].reshape(batch, w_buf, KV_B, DH_B)
    og2 = _swa_prompt(pb, sinks, batch, seq)
    y_prompt = _outproj(h1, og2, None, w_out_b0, nf, tm=512).reshape(batch, seq, d)

    ts = db * ds
    xs = x_sample.reshape(ts, d)
    pas, k_rows_s, v_rows_s = _inproj(xs, norm_a[0:1], w_in_a0, tm=ts, tn=2 * wkv_a, kv_rows=True)
    a_k_sample = k_rows_s.reshape(1, db, ds, KV_A, DH_A)
    a_v_sample = v_rows_s.reshape(1, db, ds, KV_A, DH_A)
    q_s = pas[:, q_lo:g_lo].reshape(db, ds, KV_A, G_A, DH_A).transpose(0, 2, 3, 1, 4).reshape(
        db, KV_A, G_A * ds, DH_A)
    kn_s = k_rows_s.reshape(db, ds, KV_A, DH_A).transpose(0, 2, 1, 3)
    vn_s = v_rows_s.reshape(db, ds, KV_A, DH_A).transpose(0, 2, 1, 3)
    o_s = _sb_sample(q_s, kn_s, vn_s,
                     cache_a_k[0].reshape(n_pool * page * KV_A, DH_A),
                     cache_a_v[0].reshape(n_pool * page * KV_A, DH_A),
                     page_table)
    o_s = o_s.reshape(db, KV_A, G_A, ds, DH_A).transpose(0, 3, 1, 2, 4).reshape(ts, wq_a)
    h1s = _outproj(xs, o_s, pas[:, g_lo:], w_out_a0, None, tm=ts)
    pbs = _inproj(h1s, gains_b, w_in_bkv, tm=ts, tn=tn_b, split=split_b)[0]
    k_new_b = pbs[:, 2 * wq_b:2 * wq_b + wkv_b].reshape(db, ds, wkv_b)
    v_new_b = pbs[:, 2 * wq_b + wkv_b:].reshape(db, ds, wkv_b)
    q_bs = pbs[:, :wq_b].reshape(db, ds, KV_B, G_B, DH_B).transpose(0, 2, 3, 1, 4).reshape(db, KV_B, G_B * ds, DH_B)
    kt_b = cache_b_k.transpose(0, 2, 3, 1).reshape(db * wkv_b, w_buf)
    vt_b = cache_b_v.transpose(0, 2, 3, 1).reshape(db * wkv_b, w_buf)
    o_bs = _swa_sample(q_bs, kt_b, vt_b, k_new_b, v_new_b, sinks)
    o_bs = o_bs.reshape(db, KV_B, G_B, ds, DH_B).transpose(0, 3, 1, 2, 4).reshape(ts, wq_b)
    y_sample = _outproj(h1s, o_bs, pbs, w_out_b0, nf, tm=ts, gate_col_block=1).reshape(db, ds, d)

    kb_all = jnp.concatenate([cache_b_k, k_new_b.reshape(db, ds, KV_B, DH_B)], axis=1)
    vb_all = jnp.concatenate([cache_b_v, v_new_b.reshape(db, ds, KV_B, DH_B)], axis=1)
    b_k_sample = kb_all[:, -w_buf:]
    b_v_sample = vb_all[:, -w_buf:]

    return (y_prompt, y_sample, a_k_prompt, a_v_prompt, a_k_sample, a_v_sample,
            b_k_prompt, b_v_prompt, b_k_sample, b_v_sample)
```

```python
import functools

import jax
import jax.numpy as jnp
from jax import lax
from jax.experimental import pallas as pl
from jax.experimental.pallas import tpu as pltpu

F32 = jnp.float32
BF16 = jnp.bfloat16

DH_A = 128
KV_A = 4
G_A = 4
DH_B = 64
KV_B = 4
G_B = 8
H_B = KV_B * G_B
WINDOW = 128
EPS = 1e-6
NEG = -1e30

VMEM_LIMIT_BYTES = 56 * 1024 * 1024

PROJ_TM = 1024
OUT_TM = 512
SB_TQ = 256
SB_TK = 256
SB_PAGES_PER_STEP = 16
SB_RING = 3
PAGE = 128
PAGE_ROWS = PAGE * KV_A
SWA_REQ_PER_STEP = 8
NORM_ROWS = 64
NORM_UNROLL = 4


def _cparams(sem):
    return pltpu.CompilerParams(dimension_semantics=sem, vmem_limit_bytes=VMEM_LIMIT_BYTES)


def _nt_dot(a, b):
    return lax.dot_general(a, b, (((1,), (1,)), ((), ())), preferred_element_type=F32)


def _dot(a, b):
    return jnp.dot(a, b, preferred_element_type=F32)


def _silu(g):
    return g * (1.0 / (1.0 + jnp.exp(-g)))


def _suffix_matrix(n):
    row = lax.broadcasted_iota(jnp.int32, (n, n), 0)
    col = lax.broadcasted_iota(jnp.int32, (n, n), 1)
    return jnp.where(row > col, 1.0, 0.0).astype(BF16)


LOG2E = 1.4426950408889634


def _log2_beta_terms(z2):
    lo = jnp.minimum(z2, 0.0)
    hi = lo - z2
    l = jnp.log2(1.0 + jnp.exp2(lo + hi))
    return lo - l, hi - l


def _inproj_kernel(x_ref, g_ref, w_ref, *refs, n_gains, split, kv_rows):
    o_ref, xn_ref = refs[0], refs[-1]
    j = pl.program_id(1)
    tm = x_ref.shape[0]

    @pl.when(j == 0)
    def _():
        nr = min(NORM_ROWS, tm)
        assert tm % nr == 0

        def norm_rows(c, carry):
            rows = pl.ds(pl.multiple_of(c * nr, nr), nr)
            x = x_ref[rows, :]
            xh = x * lax.rsqrt(jnp.mean(x * x, axis=-1, keepdims=True) + EPS)
            for gi in range(n_gains):
                xn_ref[gi, rows, :] = (xh * g_ref[gi:gi + 1, :]).astype(BF16)
            return carry

        lax.fori_loop(0, tm // nr, norm_rows, 0, unroll=min(NORM_UNROLL, tm // nr))

    if n_gains == 1:
        res = _dot(xn_ref[0], w_ref[...])
        o_ref[...] = res
        if kv_rows:
            k_rows_ref, v_rows_ref = refs[1], refs[2]
            wkv = KV_A * DH_A

            @pl.when(j == 0)
            def _():
                for h in range(KV_A):
                    k_rows_ref[pl.ds(h, tm, stride=KV_A), :] = res[:, h * DH_A:(h + 1) * DH_A]
                    v_rows_ref[pl.ds(h, tm, stride=KV_A), :] = res[:, wkv + h * DH_A:wkv + (h + 1) * DH_A]
    else:
        tn = o_ref.shape[1]
        split_tile, off = divmod(split, tn)

        @pl.when(j < split_tile)
        def _():
            o_ref[...] = _dot(xn_ref[0], w_ref[...])

        @pl.when(j == split_tile)
        def _():
            if off:
                o_ref[:, :off] = _dot(xn_ref[0], w_ref[:, :off])
            o_ref[:, off:] = _dot(xn_ref[1], w_ref[:, off:])

        @pl.when(j > split_tile)
        def _():
            o_ref[...] = _dot(xn_ref[1], w_ref[...])


def _inproj(x, gains, w, *, tm, tn, split=0, kv_rows=False):
    t, d = x.shape
    n = w.shape[1]
    n_gains = gains.shape[0]
    kern = functools.partial(_inproj_kernel, n_gains=n_gains, split=split, kv_rows=kv_rows)
    out_specs = [pl.BlockSpec((tm, tn), lambda i, j: (i, j))]
    out_shape = [jax.ShapeDtypeStruct((t, n), F32)]
    if kv_rows:
        assert n_gains == 1 and tn == 2 * KV_A * DH_A
        for _ in range(2):
            out_specs.append(pl.BlockSpec((tm * KV_A, DH_A), lambda i, j: (i, 0)))
            out_shape.append(jax.ShapeDtypeStruct((t * KV_A, DH_A), F32))
    return pl.pallas_call(
        kern,
        grid=(t // tm, n // tn),
        in_specs=[
            pl.BlockSpec((tm, d), lambda i, j: (i, 0)),
            pl.BlockSpec((n_gains, d), lambda i, j: (0, 0)),
            pl.BlockSpec((d, tn), lambda i, j: (0, j)),
        ],
        out_specs=out_specs,
        out_shape=out_shape,
        scratch_shapes=[pltpu.VMEM((n_gains, tm, d), BF16)],
        compiler_params=_cparams(("parallel", "arbitrary")),
        name="inproj",
    )(x, gains, w)


def _outproj_kernel(*refs, gated, final_norm):
    refs = list(refs)
    h_ref = refs.pop(0)
    o_ref = refs.pop(0)
    g_ref = refs.pop(0) if gated else None
    w_ref = refs.pop(0)
    nf_ref = refs.pop(0) if final_norm else None
    out_ref = refs.pop(0)

    if gated:
        og = (o_ref[...] * _silu(g_ref[...])).astype(BF16)
    else:
        og = o_ref[...]
    h = h_ref[...] + _dot(og, w_ref[...])
    if final_norm:
        h = h * lax.rsqrt(jnp.mean(h * h, axis=-1, keepdims=True) + EPS) * nf_ref[...]
    out_ref[...] = h


def _outproj(h, o, gate, w, norm_f, *, tm, gate_col_block=0):
    t, d = h.shape
    gated = gate is not None
    final_norm = norm_f is not None
    args = [h, o]
    in_specs = [
        pl.BlockSpec((tm, d), lambda i: (i, 0)),
        pl.BlockSpec((tm, d), lambda i: (i, 0)),
    ]
    if gated:
        args.append(gate)
        in_specs.append(pl.BlockSpec((tm, d), lambda i: (i, gate_col_block)))
    args.append(w)
    in_specs.append(pl.BlockSpec((d, d), lambda i: (0, 0)))
    if final_norm:
        args.append(norm_f)
        in_specs.append(pl.BlockSpec((1, d), lambda i: (0, 0)))
    kern = functools.partial(_outproj_kernel, gated=gated, final_norm=final_norm)
    return pl.pallas_call(
        kern,
        grid=(t // tm,),
        in_specs=in_specs,
        out_specs=pl.BlockSpec((tm, d), lambda i: (i, 0)),
        out_shape=jax.ShapeDtypeStruct((t, d), F32),
        compiler_params=_cparams(("parallel",)),
        name="outproj",
    )(*args)


def _sb_prompt_kernel(q_ref, k_ref, v_ref, g_ref, o_ref, acc_ref, carry_ref):
    suffix = _suffix_matrix(SB_TK)
    rows = G_A * SB_TQ
    nq = q_ref.shape[0] // SB_TQ

    hq = SB_TQ // 2
    assert SB_TK == SB_TQ
    tri = lax.broadcasted_iota(jnp.int32, (hq, hq), 1) < lax.broadcasted_iota(jnp.int32, (hq, hq), 0)
    zero_q = jnp.zeros((hq, hq), F32)

    def diag_blocks(x, g):
        r0 = g * SB_TQ
        return x[r0:r0 + hq, :hq], x[r0 + hq:r0 + SB_TQ, :hq], x[r0 + hq:r0 + SB_TQ, hq:]

    def assemble(blocks):
        return jnp.concatenate([
            jnp.concatenate([jnp.concatenate([a, zero_q], axis=1), jnp.concatenate([b, c], axis=1)], axis=0)
            for a, b, c in blocks], axis=0)

    def diag_terms(zt):
        ls_blocks, l1m_blocks = [], []
        for g in range(G_A):
            (lsa, la), (lsb, lb), (lsc, lc) = [_log2_beta_terms(zb) for zb in diag_blocks(zt, g)]
            ls_blocks.append((lsa, lsb, lsc))
            l1m_blocks.append((jnp.where(tri, la, 0.0), lb, jnp.where(tri, lc, 0.0)))
        return ls_blocks, assemble(l1m_blocks)

    def diag_weights(ls_blocks, after_t, carry):
        out = []
        for g in range(G_A):
            aa, ab, ac = diag_blocks(after_t, g)
            r0 = g * SB_TQ
            c_early, c_late = carry[r0:r0 + hq], carry[r0 + hq:r0 + SB_TQ]
            lsa, lsb, lsc = ls_blocks[g]
            out.append((jnp.where(tri, jnp.exp2(lsa + aa + c_early), 0.0),
                        jnp.exp2(lsb + ab + c_late),
                        jnp.where(tri, jnp.exp2(lsc + ac + c_late), 0.0)))
        return assemble(out)

    def q_tile(qi, unused):
        qrows = pl.ds(pl.multiple_of(qi * SB_TQ, SB_TQ), SB_TQ)
        q = q_ref[qrows, :] * (DH_A ** -0.5 * LOG2E)
        qs = jnp.concatenate([q[:, g * DH_A:(g + 1) * DH_A] for g in range(G_A)], axis=0).astype(BF16)

        def tiles(kb_new, n, masked):
            start = pl.multiple_of((kb_new - (n - 1)) * SB_TK, SB_TK)
            k = k_ref[pl.ds(start, n * SB_TK), :].astype(BF16)
            v = v_ref[pl.ds(start, n * SB_TK), :].astype(BF16)
            z = _nt_dot(qs, k)
            order = list(reversed(range(n)))
            ls, l1m, after = {}, {}, {}
            for t in order:
                zt = z[:, t * SB_TK:(t + 1) * SB_TK]
                if masked and t == n - 1:
                    ls[t], l1m[t] = diag_terms(zt)
                else:
                    ls[t], l1m[t] = _log2_beta_terms(zt)
                after[t] = _dot(l1m[t].astype(BF16), suffix)
            carry = carry_ref[...]
            w = [None] * n
            for t in order:
                if masked and t == n - 1:
                    wt = diag_weights(ls[t], after[t], carry)
                else:
                    wt = jnp.exp2(ls[t] + after[t] + carry)
                w[t] = wt.astype(BF16)
                carry = carry + (after[t][:, 0:1] + l1m[t][:, 0:1])
            carry_ref[...] = carry
            acc_ref[...] += _dot(jnp.concatenate(w, axis=1) if n > 1 else w[0], v)

        acc_ref[...] = jnp.zeros_like(acc_ref)
        carry_ref[...] = jnp.zeros_like(carry_ref)
        odd = qi & 1

        @pl.when(odd == 0)
        def _():
            tiles(qi, 1, True)

        @pl.when(odd == 1)
        def _():
            tiles(qi, 2, True)

        first = qi - 1 - odd

        def body(it, c):
            tiles(first - 2 * it, 2, False)
            return c

        lax.fori_loop(0, lax.shift_right_logical(qi, 1), body, 0)

        gate = _silu(g_ref[qrows, :])
        acc = acc_ref[...]
        o_ref[qrows, :] = (jnp.concatenate(
            [acc[g * SB_TQ:(g + 1) * SB_TQ] for g in range(G_A)], axis=1) * gate).astype(BF16)
        return unused

    lax.fori_loop(0, nq, q_tile, 0)


def _sb_prompt(pa, batch, seq):
    wq = G_A * DH_A
    k_col0 = 0
    v_col0 = KV_A
    q_col0 = (2 * KV_A * DH_A) // wq
    g_col0 = q_col0 + KV_A
    return pl.pallas_call(
        _sb_prompt_kernel,
        grid=(batch, KV_A),
        in_specs=[
            pl.BlockSpec((seq, wq), lambda b, h: (b, q_col0 + h)),
            pl.BlockSpec((seq, DH_A), lambda b, h: (b, k_col0 + h)),
            pl.BlockSpec((seq, DH_A), lambda b, h: (b, v_col0 + h)),
            pl.BlockSpec((seq, wq), lambda b, h: (b, g_col0 + h)),
        ],
        out_specs=pl.BlockSpec((seq, wq), lambda b, h: (b, h)),
        out_shape=jax.ShapeDtypeStruct((batch * seq, KV_A * wq), BF16),
        scratch_shapes=[
            pltpu.VMEM((G_A * SB_TQ, DH_A), F32),
            pltpu.VMEM((G_A * SB_TQ, 1), F32),
        ],
        compiler_params=_cparams(("parallel", "parallel")),
        name="sb_prompt",
    )(pa, pa, pa, pa)


def _sb_sample_kernel(pt_ref, q_ref, kn_ref, vn_ref, ck_hbm, cv_hbm, o_ref,
                      kbuf, vbuf, sems, acc_ref, carry_ref, *, n_new, n_pages):
    npg = SB_PAGES_PER_STEP
    r = pl.program_id(0)
    c = pl.program_id(1)
    n_chunks = pl.num_programs(1)
    step = r * n_chunks + c
    total = pl.num_programs(0) * n_chunks
    rows_h = G_A * n_new
    rows = KV_A * rows_h
    scale = DH_A ** -0.5 * LOG2E

    def page_copies(s, slot):
        rr = lax.div(s, n_chunks)
        cc = lax.rem(s, n_chunks)
        copies = []
        for i in range(npg):
            pid = pt_ref[rr, n_pages - 1 - (cc * npg + i)]
            src = pl.ds(pl.multiple_of(pid * PAGE_ROWS, PAGE_ROWS), PAGE_ROWS)
            dst = pl.ds(i * PAGE_ROWS, PAGE_ROWS)
            copies.append(pltpu.make_async_copy(ck_hbm.at[src, :], kbuf.at[slot, dst, :], sems.at[0, slot]))
            copies.append(pltpu.make_async_copy(cv_hbm.at[src, :], vbuf.at[slot, dst, :], sems.at[1, slot]))
        return copies

    @pl.when(step == 0)
    def _():
        for s0 in range(SB_RING - 1):
            for cp in page_copies(jnp.int32(s0), s0):
                cp.start()

    ahead = step + (SB_RING - 1)

    @pl.when(ahead < total)
    def _():
        for cp in page_copies(ahead, lax.rem(ahead, SB_RING)):
            cp.start()

    slot = lax.rem(step, SB_RING)
    for cp in page_copies(step, slot):
        cp.wait()
    kpages = kbuf.at[slot]
    vpages = vbuf.at[slot]

    @pl.when(c == 0)
    def _():
        t_of_row = lax.broadcasted_iota(jnp.int32, (rows_h, 1), 0) & (n_new - 1)
        for h in range(KV_A):
            qh = q_ref[0, h] * scale
            kn = kn_ref[0, h]
            vn = vn_ref[0, h]
            ls_cols, l1m_cols = [], []
            for j in range(n_new):
                zj = jnp.sum(qh * kn[j:j + 1, :], axis=-1, keepdims=True)
                ls_j, l1m_j = _log2_beta_terms(zj)
                ls_cols.append(ls_j)
                l1m_cols.append(jnp.where(t_of_row > j, l1m_j, 0.0))
            acc = jnp.zeros((rows_h, DH_A), F32)
            after = jnp.zeros((rows_h, 1), F32)
            for j in reversed(range(n_new)):
                wj = jnp.where(t_of_row > j, jnp.exp2(ls_cols[j] + after), 0.0)
                acc = acc + wj * vn[j:j + 1, :]
                after = after + l1m_cols[j]
            acc_ref[h] = acc
            carry_ref[h * rows_h:(h + 1) * rows_h, :] = jnp.broadcast_to(after, (rows_h, PAGE))

    def head_rows(pages, h):
        return pages[pl.ds(h, npg * PAGE, stride=KV_A), :].astype(BF16)

    z = [_nt_dot((q_ref[0, h] * scale).astype(BF16), head_rows(kpages, h)) for h in range(KV_A)]
    z = jnp.concatenate([z[h][:, i * PAGE:(i + 1) * PAGE] for i in range(npg) for h in range(KV_A)], axis=0)
    ls, l1m = _log2_beta_terms(z)
    suffix = jnp.concatenate([_suffix_matrix(PAGE), jnp.ones((PAGE, PAGE), BF16)], axis=1)
    sums = _dot(l1m.astype(BF16), suffix)
    after = sums[:, :PAGE]
    totals = sums[:, PAGE:]
    carry = carry_ref[...]
    ws = []
    for i in range(npg):
        sl = slice(i * rows, (i + 1) * rows)
        ws.append(jnp.exp2(ls[sl] + after[sl] + carry).astype(BF16))
        carry = carry + totals[sl]
    carry_ref[...] = carry
    for h in range(KV_A):
        w_h = jnp.concatenate([ws[i][h * rows_h:(h + 1) * rows_h] for i in range(npg)], axis=1)
        acc_ref[h] += _dot(w_h, head_rows(vpages, h))

    @pl.when(c == pl.num_programs(1) - 1)
    def _():
        o_ref[0] = acc_ref[...]


def _sb_sample(q, k_new, v_new, cache_k, cache_v, page_table):
    db, _, rows_h, _ = q.shape
    ds = k_new.shape[2]
    n_pages = page_table.shape[1]
    npg = SB_PAGES_PER_STEP
    n_chunks = n_pages // npg

    assert n_pages % npg == 0 and db * n_chunks >= SB_RING - 1
    in_specs = [
        pl.BlockSpec((1, KV_A, rows_h, DH_A), lambda r, c, pt: (r, 0, 0, 0)),
        pl.BlockSpec((1, KV_A, ds, DH_A), lambda r, c, pt: (r, 0, 0, 0)),
        pl.BlockSpec((1, KV_A, ds, DH_A), lambda r, c, pt: (r, 0, 0, 0)),
        pl.BlockSpec(memory_space=pl.ANY),
        pl.BlockSpec(memory_space=pl.ANY),
    ]
    grid_spec = pltpu.PrefetchScalarGridSpec(
        num_scalar_prefetch=1,
        grid=(db, n_chunks),
        in_specs=in_specs,
        out_specs=pl.BlockSpec((1, KV_A, rows_h, DH_A), lambda r, c, pt: (r, 0, 0, 0)),
        scratch_shapes=[
            pltpu.VMEM((SB_RING, npg * PAGE_ROWS, DH_A), F32),
            pltpu.VMEM((SB_RING, npg * PAGE_ROWS, DH_A), F32),
            pltpu.SemaphoreType.DMA((2, SB_RING)),
            pltpu.VMEM((KV_A, rows_h, DH_A), F32),
            pltpu.VMEM((KV_A * rows_h, PAGE), F32),
        ],
    )
    kern = functools.partial(_sb_sample_kernel, n_new=ds, n_pages=n_pages)
    return pl.pallas_call(
        kern,
        grid_spec=grid_spec,
        out_shape=jax.ShapeDtypeStruct((db, KV_A, rows_h, DH_A), F32),
        compiler_params=_cparams(("arbitrary", "arbitrary")),
        name="sb_sample",
    )(page_table, q, k_new, v_new, cache_k, cache_v)


def _head_columns(sinks_ref, kvh, rows_per_head):
    g_idx = lax.broadcasted_iota(jnp.int32, (G_B * rows_per_head, 1), 0) // rows_per_head
    slope = jnp.zeros((G_B * rows_per_head, 1), F32)
    sink = jnp.zeros((G_B * rows_per_head, 1), F32)
    for g in range(G_B):
        h = kvh * G_B + g
        slope = jnp.where(g_idx == g, 2.0 ** (-8.0 * (h + 1) / H_B), slope)
        sink = jnp.where(g_idx == g, sinks_ref[h], sink)
    return slope, sink


def _swa_prompt_kernel(sinks_ref, q_ref, g_ref, kvp_ref, kvc_ref, bias_ref, o_ref):
    n = pl.program_id(1)
    blk = WINDOW
    pairs = G_B // 2
    q = (q_ref[...] * (DH_B ** -0.5 * LOG2E)).astype(BF16)
    kv = jnp.concatenate([kvp_ref[...], kvc_ref[...]], axis=0)
    no_prev = jnp.where(n > 0, 0.0, NEG)
    low_kv = lax.broadcasted_iota(jnp.int32, (2 * blk, 2 * DH_B), 1) < DH_B
    low_o = lax.broadcasted_iota(jnp.int32, (blk, 2 * DH_B), 1) < DH_B
    vo = KV_B * DH_B
    outs = []
    for kvh in range(KV_B):
        grp = (kvh // 2) * 2 * DH_B
        own_low = kvh % 2 == 0
        mask = low_kv if own_low else jnp.logical_not(low_kv)
        k_own = jnp.where(mask, kv[:, grp:grp + 2 * DH_B], 0.0)
        v_own = jnp.where(mask, kv[:, vo + grp:vo + grp + 2 * DH_B], 0.0)
        k_oth = pltpu.roll(k_own, DH_B, 1)
        v_oth = pltpu.roll(v_own, DH_B, 1)
        k_lo, k_hi = (k_own, k_oth) if own_low else (k_oth, k_own)
        v_lo, v_hi = (v_own, v_oth) if own_low else (v_oth, v_own)
        qp = jnp.concatenate(
            [q[:, (kvh * pairs + pr) * 2 * DH_B:(kvh * pairs + pr + 1) * 2 * DH_B] for pr in range(pairs)], axis=0)
        z = (_nt_dot(qp, k_lo.astype(BF16)), _nt_dot(qp, k_hi.astype(BF16)))
        ps, inv = ([], []), ([], [])
        for pr in range(pairs):
            for half in range(2):
                h = kvh * G_B + 2 * pr + half
                s = z[half][pr * blk:(pr + 1) * blk] + bias_ref[h]
                s = jnp.concatenate([s[:, :blk] + no_prev, s[:, blk:]], axis=1)
                sink = sinks_ref[h] * LOG2E
                m = jnp.maximum(jnp.max(s, axis=-1, keepdims=True), sink)
                p = jnp.exp2(s - m)
                inv[half].append(1.0 / (jnp.sum(p, axis=-1, keepdims=True) + jnp.exp2(sink - m)))
                ps[half].append(p.astype(BF16))
        o = (_dot(jnp.concatenate(ps[0], axis=0), v_lo.astype(BF16))
             + _dot(jnp.concatenate(ps[1], axis=0), v_hi.astype(BF16)))
        for pr in range(pairs):
            outs.append(o[pr * blk:(pr + 1) * blk] * jnp.where(low_o, inv[0][pr], inv[1][pr]))
    o_ref[...] = (jnp.concatenate(outs, axis=1) * _silu(g_ref[...])).astype(BF16)


def _swa_bias():
    i = jnp.arange(WINDOW)[:, None]
    j = jnp.arange(2 * WINDOW)[None, :]
    dist = i - j + WINDOW
    band = (dist >= 0) & (dist < WINDOW)
    slopes = jnp.exp2(-8.0 * jnp.arange(1, H_B + 1, dtype=F32) / H_B)
    return jnp.where(band[None], -slopes[:, None, None] * dist[None].astype(F32) * LOG2E, NEG)


def _swa_prompt(pb, sinks, batch, seq):
    blk = WINDOW
    nb = seq // blk
    wq = H_B * DH_B
    wkv = 2 * KV_B * DH_B
    kv_col = (2 * wq) // wkv
    return pl.pallas_call(
        _swa_prompt_kernel,
        grid=(batch, nb),
        in_specs=[
            pl.BlockSpec(memory_space=pltpu.SMEM),
            pl.BlockSpec((blk, wq), lambda b, n: (b * nb + n, 0)),
            pl.BlockSpec((blk, wq), lambda b, n: (b * nb + n, 1)),
            pl.BlockSpec((blk, wkv), lambda b, n: (b * nb + jnp.maximum(n - 1, 0), kv_col)),
            pl.BlockSpec((blk, wkv), lambda b, n: (b * nb + n, kv_col)),
            pl.BlockSpec((H_B, blk, 2 * blk), lambda b, n: (0, 0, 0)),
        ],
        out_specs=pl.BlockSpec((blk, wq), lambda b, n: (b * nb + n, 0)),
        out_shape=jax.ShapeDtypeStruct((batch * seq, wq), BF16),
        compiler_params=_cparams(("parallel", "arbitrary")),
        name="swa_prompt",
    )(sinks, pb, pb, pb, pb, _swa_bias())


def _swa_sample_kernel(sinks_ref, q_ref, kct_ref, vct_ref, kn_ref, vn_ref, o_ref, *, n_new):
    n_buf = kct_ref.shape[1]
    rows = G_B * n_new
    keys = 2 * n_buf
    t = lax.broadcasted_iota(jnp.int32, (rows, keys), 0) & (n_new - 1)
    j = lax.broadcasted_iota(jnp.int32, (rows, keys), 1)
    dist = n_buf + t - j
    valid = (dist >= 0) & (dist < WINDOW)
    distf = dist.astype(F32)
    pad = jnp.zeros((n_buf - kn_ref.shape[1], DH_B), F32)
    for r in range(SWA_REQ_PER_STEP):
        kn = kn_ref[r]
        vn = vn_ref[r]
        for kvh in range(KV_B):
            c0 = kvh * DH_B
            base = (r * KV_B + kvh) * DH_B
            kct = kct_ref[base:base + DH_B, :].astype(BF16)
            vct = vct_ref[base:base + DH_B, :].astype(BF16)
            knp = jnp.concatenate([kn[:, c0:c0 + DH_B], pad], axis=0).astype(BF16)
            vnp = jnp.concatenate([vn[:, c0:c0 + DH_B], pad], axis=0).astype(BF16)
            qs = (q_ref[r, kvh] * (DH_B ** -0.5)).astype(BF16)
            slope, sink = _head_columns(sinks_ref, kvh, n_new)
            s = jnp.concatenate([_dot(qs, kct), _nt_dot(qs, knp)], axis=1)
            s = jnp.where(valid, s - slope * distf, NEG)
            m = jnp.maximum(jnp.max(s, axis=-1, keepdims=True), sink)
            p = jnp.exp(s - m)
            denom = jnp.sum(p, axis=-1, keepdims=True) + jnp.exp(sink - m)
            pb = p.astype(BF16)
            o = _nt_dot(pb[:, :n_buf], vct) + _dot(pb[:, n_buf:], vnp)
            o_ref[r, kvh] = o * (1.0 / denom)


def _swa_sample(q, cache_kt, cache_vt, k_new, v_new, sinks):
    db, _, rows, _ = q.shape
    n_buf = cache_kt.shape[1]
    ds = k_new.shape[1]
    wkv = KV_B * DH_B
    rq = SWA_REQ_PER_STEP
    ds_pad = -(-ds // 8) * 8
    k_new = jnp.pad(k_new, ((0, 0), (0, ds_pad - ds), (0, 0)))
    v_new = jnp.pad(v_new, ((0, 0), (0, ds_pad - ds), (0, 0)))
    kern = functools.partial(_swa_sample_kernel, n_new=ds)
    return pl.pallas_call(
        kern,
        grid=(db // rq,),
        in_specs=[
            pl.BlockSpec(memory_space=pltpu.SMEM),
            pl.BlockSpec((rq, KV_B, rows, DH_B), lambda i: (i, 0, 0, 0)),
            pl.BlockSpec((rq * wkv, n_buf), lambda i: (i, 0)),
            pl.BlockSpec((rq * wkv, n_buf), lambda i: (i, 0)),
            pl.BlockSpec((rq, ds_pad, wkv), lambda i: (i, 0, 0)),
            pl.BlockSpec((rq, ds_pad, wkv), lambda i: (i, 0, 0)),
        ],
        out_specs=pl.BlockSpec((rq, KV_B, rows, DH_B), lambda i: (i, 0, 0, 0)),
        out_shape=jax.ShapeDtypeStruct((db, KV_B, rows, DH_B), F32),
        compiler_params=_cparams(("parallel",)),
        name="swa_sample",
    )(sinks, q, cache_kt, cache_vt, k_new, v_new)


def kernel(x_prompt, x_sample, cache_a_k, cache_a_v, cache_b_k, cache_b_v, page_table,
           norm_a, w_in_a, w_out_a, norm_kv, w_kv, norm_b, w_in_b, sinks_b, w_out_b, norm_f):
    batch, seq, d = x_prompt.shape
    db, ds, _ = x_sample.shape
    w_buf = cache_b_k.shape[1]
    n_pool, page = cache_a_k.shape[1], cache_a_k.shape[2]
    wq_a = KV_A * G_A * DH_A
    wkv_a = KV_A * DH_A
    wq_b = H_B * DH_B
    wkv_b = KV_B * DH_B

    w_a = w_in_a[0]
    w_in_a0 = jnp.concatenate([w_a[:, wq_a:wq_a + 2 * wkv_a], w_a[:, :wq_a], w_a[:, wq_a + 2 * wkv_a:]],
                              axis=1).astype(BF16)
    q_lo, g_lo = 2 * wkv_a, 2 * wkv_a + wq_a
    w_out_a0 = w_out_a[0].astype(BF16)
    w_in_bkv = jnp.concatenate([w_in_b[0], w_kv], axis=1).astype(BF16)
    w_out_b0 = w_out_b[0].astype(BF16)
    gains_b = jnp.stack([norm_b[0], norm_kv], axis=0)
    nf = norm_f.reshape(1, d)
    sinks = sinks_b[0]
    split_b = 2 * wq_b
    tn_b = (2 * wq_b + 2 * wkv_b) // 3

    xp = x_prompt.reshape(batch * seq, d)
    pa, k_rows, v_rows = _inproj(xp, norm_a[0:1], w_in_a0, tm=PROJ_TM, tn=2 * wkv_a, kv_rows=True)
    a_k_prompt = k_rows.reshape(1, batch, seq, KV_A, DH_A)
    a_v_prompt = v_rows.reshape(1, batch, seq, KV_A, DH_A)
    og = _sb_prompt(pa, batch, seq)
    h1 = _outproj(xp, og, None, w_out_a0, None, tm=OUT_TM)
    pb = _inproj(h1, gains_b, w_in_bkv, tm=PROJ_TM, tn=tn_b, split=split_b)[0]
    kvb = pb.reshape(batch, seq, 2 * wq_b + 2 * wkv_b)[:, seq - w_buf:, 2 * wq_b:]
    b_k_prompt = kvb[:, :, :wkv_b].reshape(batch, w_buf, KV_B, DH_B)
    b_v_prompt = kvb[:, :, wkv_b:].reshape(batch, w_buf, KV_B, DH_B)
    og2 = _swa_prompt(pb, sinks, batch, seq)
    y_prompt = _outproj(h1, og2, None, w_out_b0, nf, tm=OUT_TM).reshape(batch, seq, d)

    ts = db * ds
    xs = x_sample.reshape(ts, d)
    pas, k_rows_s, v_rows_s = _inproj(xs, norm_a[0:1], w_in_a0, tm=ts, tn=2 * wkv_a, kv_rows=True)
    a_k_sample = k_rows_s.reshape(1, db, ds, KV_A, DH_A)
    a_v_sample = v_rows_s.reshape(1, db, ds, KV_A, DH_A)
    q_s = pas[:, q_lo:g_lo].reshape(db, ds, KV_A, G_A, DH_A).transpose(0, 2, 3, 1, 4).reshape(
        db, KV_A, G_A * ds, DH_A)
    kn_s = k_rows_s.reshape(db, ds, KV_A, DH_A).transpose(0, 2, 1, 3)
    vn_s = v_rows_s.reshape(db, ds, KV_A, DH_A).transpose(0, 2, 1, 3)
    o_s = _sb_sample(q_s, kn_s, vn_s,
                     cache_a_k[0].reshape(n_pool * page * KV_A, DH_A),
                     cache_a_v[0].reshape(n_pool * page * KV_A, DH_A),
                     page_table)
    o_s = o_s.reshape(db, KV_A, G_A, ds, DH_A).transpose(0, 3, 1, 2, 4).reshape(ts, wq_a)
    h1s = _outproj(xs, o_s, pas[:, g_lo:], w_out_a0, None, tm=ts)
    pbs = _inproj(h1s, gains_b, w_in_bkv, tm=ts, tn=tn_b, split=split_b)[0]
    k_new_b = pbs[:, 2 * wq_b:2 * wq_b + wkv_b].reshape(db, ds, wkv_b)
    v_new_b = pbs[:, 2 * wq_b + wkv_b:].reshape(db, ds, wkv_b)
    q_bs = pbs[:, :wq_b].reshape(db, ds, KV_B, G_B, DH_B).transpose(0, 2, 3, 1, 4).reshape(db, KV_B, G_B * ds, DH_B)
    kt_b = cache_b_k.transpose(0, 2, 3, 1).reshape(db * wkv_b, w_buf)
    vt_b = cache_b_v.transpose(0, 2, 3, 1).reshape(db * wkv_b, w_buf)
    o_bs = _swa_sample(q_bs, kt_b, vt_b, k_new_b, v_new_b, sinks)
    o_bs = o_bs.reshape(db, KV_B, G_B, ds, DH_B).transpose(0, 3, 1, 2, 4).reshape(ts, wq_b)
    y_sample = _outproj(h1s, o_bs, pbs, w_out_b0, nf, tm=ts, gate_col_block=1).reshape(db, ds, d)

    kb_all = jnp.concatenate([cache_b_k, k_new_b.reshape(db, ds, KV_B, DH_B)], axis=1)
    vb_all = jnp.concatenate([cache_b_v, v_new_b.reshape(db, ds, KV_B, DH_B)], axis=1)
    b_k_sample = kb_all[:, -w_buf:]
    b_v_sample = vb_all[:, -w_buf:]

    return (y_prompt, y_sample, a_k_prompt, a_v_prompt, a_k_sample, a_v_sample,
            b_k_prompt, b_v_prompt, b_k_sample, b_v_sample)
```
